```python
import math, functools
import jax, jax.numpy as jnp
from jax import lax
import numpy as np

D_MODEL = 1024
BATCH = 4
SEQ = 8192
DEPTH = 2

N_EVEN = (DEPTH + 1) // 2
N_ODD = DEPTH // 2
RMS_EPS = 1e-6

MLA_HEADS = 8
MLA_Q_RANK = 256
MLA_KV_RANK = 128
MLA_NOPE = 64
MLA_ROPE = 32
MLA_V = 64
ROPE_THETA = 10000.0
Q_BLOCK = 128

POOL_WINDOWS = (2, 4, 8, 16)
POOL_GROUP = 128
POOL_WIDTH = POOL_GROUP * len(POOL_WINDOWS)
MIX_AB = MLA_HEADS * MLA_V + POOL_WIDTH
IN_AB = MLA_Q_RANK + MLA_KV_RANK + MLA_ROPE + POOL_WIDTH + MIX_AB

GDN_HEADS = 8
GDN_DK = 128
GDN_DV = 128
CONV_WIDTH = 4
CHUNK = 64
GDN_QK = GDN_HEADS * GDN_DK
GDN_VW = GDN_HEADS * GDN_DV
GDN_CONV_CH = 2 * GDN_QK + GDN_VW
IN_C = GDN_CONV_CH + GDN_VW + 2 * GDN_HEADS

kernel_name = "hybrid_mla_pool_gdn_gated"


def rmsnorm(x, g):
    xf = x.astype(jnp.float32)
    y = xf * lax.rsqrt(jnp.mean(xf * xf, axis=-1, keepdims=True) + RMS_EPS)
    return (y * g.astype(jnp.float32)).astype(x.dtype)


def l2norm(x):
    xf = x.astype(jnp.float32)
    return xf * lax.rsqrt(jnp.sum(xf * xf, axis=-1, keepdims=True) + RMS_EPS)


def rope_tables(positions):
    half = MLA_ROPE // 2
    inv_freq = 1.0 / (ROPE_THETA ** (jnp.arange(half, dtype=jnp.float32) / half))
    ang = positions.astype(jnp.float32)[..., None] * inv_freq
    return jnp.cos(ang), jnp.sin(ang)


def apply_rope(x, cos, sin):
    half = MLA_ROPE // 2
    xf = x.astype(jnp.float32)
    x1, x2 = xf[..., :half], xf[..., half:]
    return jnp.concatenate([x1 * cos - x2 * sin, x2 * cos + x1 * sin], axis=-1).astype(x.dtype)


def mla(q_lat, kv_lat, k_rope, positions, q_a_norm, w_q_b, kv_a_norm, w_kv_b):
    B, S, _ = q_lat.shape
    q = (rmsnorm(q_lat, q_a_norm) @ w_q_b).reshape(B, S, MLA_HEADS, MLA_NOPE + MLA_ROPE)
    q_nope, q_rope = q[..., :MLA_NOPE], q[..., MLA_NOPE:]
    kv = (rmsnorm(kv_lat, kv_a_norm) @ w_kv_b).reshape(B, S, MLA_HEADS, MLA_NOPE + MLA_V)
    k_nope, v = kv[..., :MLA_NOPE], kv[..., MLA_NOPE:]
    cos, sin = rope_tables(positions)
    q_rope = apply_rope(q_rope, cos[:, :, None, :], sin[:, :, None, :])
    k_rope = apply_rope(k_rope, cos, sin)
    scale = (MLA_NOPE + MLA_ROPE) ** -0.5
    nb = S // Q_BLOCK
    qn_b = q_nope.reshape(B, nb, Q_BLOCK, MLA_HEADS, MLA_NOPE).transpose(1, 0, 2, 3, 4)
    qr_b = q_rope.reshape(B, nb, Q_BLOCK, MLA_HEADS, MLA_ROPE).transpose(1, 0, 2, 3, 4)
    key_idx = jnp.arange(S)

    def attend_block(args):
        qn, qr, i = args
        s = (jnp.einsum('bqhd,bkhd->bhqk', qn, k_nope, preferred_element_type=jnp.float32)
             + jnp.einsum('bqhr,bkr->bhqk', qr, k_rope, preferred_element_type=jnp.float32))
        q_idx = i * Q_BLOCK + jnp.arange(Q_BLOCK)
        causal = key_idx[None, :] <= q_idx[:, None]
        p = jax.nn.softmax(jnp.where(causal, s * scale, -jnp.inf), axis=-1)
        return jnp.einsum('bhqk,bkhd->bqhd', p.astype(v.dtype), v)

    o = lax.map(attend_block, (qn_b, qr_b, jnp.arange(nb)))
    return o.transpose(1, 0, 2, 3, 4).reshape(B, S, MLA_HEADS * MLA_V)


def multiscale_pool(xp, pool_w, pool_scale):
    B, S, _ = xp.shape
    xf = xp.astype(jnp.float32)
    csum = jnp.cumsum(xf, axis=1)
    t = jnp.arange(S)
    outs = []
    for g, w in enumerate(POOL_WINDOWS):
        lo, hi = g * POOL_GROUP, (g + 1) * POOL_GROUP
        c = csum[..., lo:hi]
        lagged = jnp.pad(c, ((0, 0), (w, 0), (0, 0)))[:, :S]
        count = jnp.minimum(t + 1, w).astype(jnp.float32)[None, :, None]
        outs.append((c - lagged) / count - xf[..., lo:hi])
    d = jnp.stack(outs, axis=2).astype(xp.dtype)
    y = jnp.einsum('bsgc,gcd->bsgd', d, pool_w).reshape(B, S, POOL_WIDTH)
    return y * pool_scale


def causal_dwconv(x, w):
    S = x.shape[1]
    xp = jnp.pad(x, ((0, 0), (CONV_WIDTH - 1, 0), (0, 0)))
    y = xp[:, 0:S] * w[0]
    for j in range(1, CONV_WIDTH):
        y = y + xp[:, j:j + S] * w[j]
    return y


def chunk_gated_delta(q, k, v, g, beta):
    B, H, S, Dk = q.shape
    Dv = v.shape[-1]
    n = S // CHUNK
    q = (q * Dk ** -0.5).reshape(B, H, n, CHUNK, Dk)
    k = k.reshape(B, H, n, CHUNK, Dk)
    v = v.reshape(B, H, n, CHUNK, Dv)
    beta = beta.reshape(B, H, n, CHUNK)
    gc = jnp.cumsum(g.reshape(B, H, n, CHUNK), axis=-1)
    idx = jnp.arange(CHUNK)
    incl = idx[:, None] >= idx[None, :]
    strict = idx[:, None] > idx[None, :]
    gamma = jnp.exp(jnp.where(incl, gc[..., :, None] - gc[..., None, :], -jnp.inf))
    kb = k * beta[..., None]
    m = jnp.where(strict, jnp.einsum('bhnid,bhnjd->bhnij', kb, k) * gamma, 0.0)
    a_mat = m + jnp.eye(CHUNK, dtype=m.dtype)
    solve = functools.partial(lax.linalg.triangular_solve, left_side=True, lower=True, unit_diagonal=True)
    u = solve(a_mat, v * beta[..., None])
    w = solve(a_mat, kb * jnp.exp(gc)[..., None])
    attn = jnp.einsum('bhnid,bhnjd->bhnij', q, k) * gamma
    q_dec = q * jnp.exp(gc)[..., None]
    k_dec = k * jnp.exp(gc[..., -1:] - gc)[..., None]
    g_last = jnp.exp(gc[..., -1])

    def step(state, inp):
        u_i, w_i, qd_i, kd_i, at_i, gl_i = inp
        v_new = u_i - jnp.einsum('bhck,bhkv->bhcv', w_i, state)
        o_i = jnp.einsum('bhck,bhkv->bhcv', qd_i, state) + jnp.einsum('bhcj,bhjv->bhcv', at_i, v_new)
        state = state * gl_i[..., None, None] + jnp.einsum('bhck,bhcv->bhkv', kd_i, v_new)
        return state, o_i

    mv = lambda a: jnp.moveaxis(a, 2, 0)
    state0 = jnp.zeros((B, H, Dk, Dv), jnp.float32)
    _, o = lax.scan(step, state0, (mv(u), mv(w), mv(q_dec), mv(k_dec), mv(attn), mv(g_last)))
    return jnp.moveaxis(o, 0, 2).reshape(B, H, S, Dv)


def mla_pool_layer(h, positions, w_in, q_a_norm, w_q_b, kv_a_norm, w_kv_b, pool_w, pool_scale, w_out):
    proj = h @ w_in
    o1 = MLA_Q_RANK
    o2 = o1 + MLA_KV_RANK
    o3 = o2 + MLA_ROPE
    o4 = o3 + POOL_WIDTH
    q_lat, kv_lat, k_rope = proj[..., :o1], proj[..., o1:o2], proj[..., o2:o3]
    xp, z = proj[..., o3:o4], proj[..., o4:]
    y_a = mla(q_lat, kv_lat, k_rope, positions, q_a_norm, w_q_b, kv_a_norm, w_kv_b)
    y_b = multiscale_pool(xp, pool_w, pool_scale)
    y = jnp.concatenate([y_a, y_b], axis=-1) * jax.nn.silu(z)
    return y @ w_out


def gdn_layer(h, w_in, conv_w, a_log, dt_bias, o_norm, w_out):
    B, S, _ = h.shape
    proj = h @ w_in
    qkv = jax.nn.silu(causal_dwconv(proj[..., :GDN_CONV_CH], conv_w))
    o1 = GDN_CONV_CH + GDN_VW
    z = proj[..., GDN_CONV_CH:o1]
    a = proj[..., o1:o1 + GDN_HEADS].astype(jnp.float32)
    b = proj[..., o1 + GDN_HEADS:].astype(jnp.float32)
    q = l2norm(qkv[..., :GDN_QK].reshape(B, S, GDN_HEADS, GDN_DK))
    k = l2norm(qkv[..., GDN_QK:2 * GDN_QK].reshape(B, S, GDN_HEADS, GDN_DK))
    v = qkv[..., 2 * GDN_QK:].reshape(B, S, GDN_HEADS, GDN_DV).astype(jnp.float32)
    beta = jax.nn.sigmoid(b)
    g = -jnp.exp(a_log.astype(jnp.float32)) * jax.nn.softplus(a + dt_bias.astype(jnp.float32))
    tr = lambda t: jnp.swapaxes(t, 1, 2)
    o = chunk_gated_delta(tr(q), tr(k), tr(v), tr(g), tr(beta))
    o = rmsnorm(tr(o), o_norm) * jax.nn.silu(z.astype(jnp.float32).reshape(B, S, GDN_HEADS, GDN_DV))
    return o.reshape(B, S, GDN_VW).astype(h.dtype) @ w_out


def setup_inputs(seed: int = 0) -> dict:
    key = jax.random.key(seed)
    ks = jax.random.split(key, 24)
    f32 = jnp.float32

    def dense(k, shape, fan_in):
        return jax.random.normal(k, shape, f32) * fan_in ** -0.5

    def gain(k, shape):
        return 1.0 + 0.02 * jax.random.normal(k, shape, f32)

    x = jax.random.normal(ks[0], (BATCH, SEQ, D_MODEL), f32)
    positions = jnp.broadcast_to(jnp.arange(SEQ, dtype=jnp.int32)[None, :], (BATCH, SEQ))
    dt = jnp.exp(jax.random.uniform(ks[16], (N_ODD, GDN_HEADS), f32, math.log(1e-3), math.log(1e-1)))
    return {
        "x": x,
        "positions": positions,
        "norm_ab": gain(ks[1], (N_EVEN, D_MODEL)),
        "w_in_ab": dense(ks[2], (N_EVEN, D_MODEL, IN_AB), D_MODEL),
        "q_a_norm": gain(ks[3], (N_EVEN, MLA_Q_RANK)),
        "w_q_b": dense(ks[4], (N_EVEN, MLA_Q_RANK, MLA_HEADS * (MLA_NOPE + MLA_ROPE)), MLA_Q_RANK),
        "kv_a_norm": gain(ks[5], (N_EVEN, MLA_KV_RANK)),
        "w_kv_b": dense(ks[6], (N_EVEN, MLA_KV_RANK, MLA_HEADS * (MLA_NOPE + MLA_V)), MLA_KV_RANK),
        "pool_w": dense(ks[7], (N_EVEN, len(POOL_WINDOWS), POOL_GROUP, POOL_GROUP), POOL_GROUP),
        "pool_scale": gain(ks[8], (N_EVEN, POOL_WIDTH)),
        "w_out_ab": dense(ks[9], (N_EVEN, MIX_AB, D_MODEL), MIX_AB),
        "norm_c": gain(ks[10], (N_ODD, D_MODEL)),
        "w_in_c": dense(ks[11], (N_ODD, D_MODEL, IN_C), D_MODEL),
        "conv_w": dense(ks[12], (N_ODD, CONV_WIDTH, GDN_CONV_CH), CONV_WIDTH),
        "a_log": jnp.log(jax.random.uniform(ks[13], (N_ODD, GDN_HEADS), f32, 1.0, 16.0)),
        "dt_bias": dt + jnp.log(-jnp.expm1(-dt)),
        "o_norm": gain(ks[14], (N_ODD, GDN_DV)),
        "w_out_c": dense(ks[15], (N_ODD, GDN_VW, D_MODEL), GDN_VW),
        "final_norm": gain(ks[17], (D_MODEL,)),
    }


def reference(x, positions, norm_ab, w_in_ab, q_a_norm, w_q_b, kv_a_norm, w_kv_b, pool_w, pool_scale,
              w_out_ab, norm_c, w_in_c, conv_w, a_log, dt_bias, o_norm, w_out_c, final_norm):
    h = x
    for layer in range(DEPTH):
        i = layer // 2
        if layer % 2 == 0:
            h = h + mla_pool_layer(rmsnorm(h, norm_ab[i]), positions, w_in_ab[i], q_a_norm[i], w_q_b[i],
                                   kv_a_norm[i], w_kv_b[i], pool_w[i], pool_scale[i], w_out_ab[i])
        else:
            h = h + gdn_layer(rmsnorm(h, norm_c[i]), w_in_c[i], conv_w[i], a_log[i], dt_bias[i],
                              o_norm[i], w_out_c[i])
    return rmsnorm(h, final_norm)
```

```python
import functools

import jax
import jax.numpy as jnp
from jax import lax
from jax.experimental import pallas as pl
from jax.experimental.pallas import tpu as pltpu

F32 = jnp.float32
BF16 = jnp.bfloat16

LANES = 128
RMS_EPS = 1e-6

MLA_HEADS = 8
MLA_Q_RANK = 256
MLA_KV_RANK = 128
MLA_NOPE = 64
MLA_ROPE = 32
MLA_V = 64
ROPE_THETA = 10000.0
ROPE_HALF = MLA_ROPE // 2

POOL_WINDOWS = (2, 4, 8, 16)
POOL_GROUP = 128
POOL_WIDTH = POOL_GROUP * len(POOL_WINDOWS)
POOL_HALO = 16

GDN_HEADS = 8
GDN_DK = 128
GDN_DV = 128
CONV_WIDTH = 4
CONV_HALO = 8
GDN_CHUNK = 64
GDN_QK = GDN_HEADS * GDN_DK
GDN_VW = GDN_HEADS * GDN_DV
GDN_CONV_CH = 2 * GDN_QK + GDN_VW

VMEM_LIMIT = 48 * 1024 * 1024
NEG_BIG = -1e30


def _tile(n, pref):
    t = min(n, pref)
    assert n % t == 0, (n, t)
    return t


def _params(sem):
    return pltpu.CompilerParams(dimension_semantics=sem, vmem_limit_bytes=VMEM_LIMIT)


def _silu(z):
    return z / (1.0 + jnp.exp(-z))


def _rms(x, gain):
    ms = jnp.mean(x * x, axis=-1, keepdims=True)
    return x * lax.rsqrt(ms + RMS_EPS) * gain


def _dot(a, b):
    return jnp.dot(a, b, preferred_element_type=F32)


def _dot_nt(a, b):
    return lax.dot_general(a, b, (((1,), (1,)), ((), ())), preferred_element_type=F32)


def _dot_tn(a, b):
    return lax.dot_general(a, b, (((0,), (0,)), ((), ())), preferred_element_type=F32)


def _dot_f32(a, b):
    return jnp.dot(a, b, preferred_element_type=F32, precision=lax.Precision.HIGHEST)


def _rope_table_kernel(pos_ref, freq_ref, cos_ref, sin_ref):
    ang = pos_ref[...].astype(F32) * freq_ref[...]
    cos_ref[...] = jnp.cos(ang)
    sin_ref[...] = jnp.sin(ang)


def _rope_tables(positions):
    b, s = positions.shape
    n = b * s * ROPE_HALF
    rows = n // LANES
    pos_rep = jnp.repeat(positions.reshape(-1), ROPE_HALF).reshape(rows, LANES)
    inv_freq = 1.0 / (ROPE_THETA ** (jnp.arange(ROPE_HALF, dtype=F32) / ROPE_HALF))
    freq_row = jnp.tile(inv_freq, LANES // ROPE_HALF).reshape(1, LANES)
    tr = _tile(rows, 512)
    cos, sin = pl.pallas_call(
        _rope_table_kernel,
        grid=(rows // tr,),
        in_specs=[pl.BlockSpec((tr, LANES), lambda i: (i, 0)),
                  pl.BlockSpec((1, LANES), lambda i: (0, 0))],
        out_specs=[pl.BlockSpec((tr, LANES), lambda i: (i, 0))] * 2,
        out_shape=[jax.ShapeDtypeStruct((rows, LANES), F32)] * 2,
        compiler_params=_params(("parallel",)),
    )(pos_rep, freq_row)
    return cos.reshape(b, s, ROPE_HALF), sin.reshape(b, s, ROPE_HALF)


def _l0_in_kernel(x_ref, cp_ref, sp_ref, nrm_ref, win_ref, qn_ref, wq_ref, kvn_ref, wkv_ref, pw_ref, ps_ref,
                  q_out, k_out, v_out, gz_out, yb_out, ext_ref, *, tm, scale):
    t = pl.program_id(1)
    hn = _rms(x_ref[0], nrm_ref[...]).astype(BF16)
    proj = _dot(hn, win_ref[...])
    o1 = MLA_Q_RANK
    o2 = o1 + MLA_KV_RANK
    o3 = o2 + LANES
    o4 = o3 + LANES
    o5 = o4 + POOL_WIDTH
    q_lat, kv_lat = proj[:, :o1], proj[:, o1:o2]
    kr, kr_sw = proj[:, o2:o3], proj[:, o3:o4]
    xp = proj[:, o4:o5]
    z = proj[:, o5:]

    lane = lax.broadcasted_iota(jnp.int32, (tm, LANES), 1)
    cpat = cp_ref[0]
    spat = jnp.where(lane < MLA_NOPE + ROPE_HALF, -sp_ref[0], sp_ref[0])

    qq = _dot(_rms(q_lat, qn_ref[...]).astype(BF16), wq_ref[...])
    hw = MLA_HEADS * LANES
    cq, sq = cpat * scale, spat * scale
    for h in range(MLA_HEADS):
        lo = h * LANES
        qh = qq[:, lo:lo + LANES] * cq + qq[:, hw + lo:hw + lo + LANES] * sq
        q_out[0, :, lo:lo + LANES] = qh.astype(BF16)

    kk = _dot(_rms(kv_lat, kvn_ref[...]).astype(BF16), wkv_ref[...])
    k_rope = kr * cpat + kr_sw * spat
    for h in range(MLA_HEADS):
        lo = h * LANES
        k_out[0, :, lo:lo + LANES] = (kk[:, lo:lo + LANES] + k_rope).astype(BF16)
    v_out[0] = kk[:, hw:].astype(BF16)

    gz_out[0] = _silu(z[:, :MLA_HEADS * MLA_V]).astype(BF16)

    @pl.when(t == 0)
    def _():
        ext_ref[0:POOL_HALO, :] = jnp.zeros((POOL_HALO, POOL_WIDTH), F32)

    ext_ref[POOL_HALO:POOL_HALO + tm, :] = xp
    row = lax.broadcasted_iota(jnp.int32, (tm, 1), 0) + t * tm
    zb = z[:, MLA_HEADS * MLA_V:]
    for g, w in enumerate(POOL_WINDOWS):
        lo = g * POOL_GROUP
        xg = xp[:, lo:lo + POOL_GROUP]
        acc = xg
        for j in range(1, w):
            acc = acc + ext_ref[POOL_HALO - j:POOL_HALO - j + tm, lo:lo + POOL_GROUP]
        cnt = jnp.minimum(row + 1, w).astype(F32)
        d = acc / cnt - xg
        y = _dot(d.astype(BF16), pw_ref[g]) * ps_ref[:, lo:lo + POOL_GROUP]
        yb_out[0, :, lo:lo + POOL_GROUP] = (y * _silu(zb[:, lo:lo + POOL_GROUP])).astype(BF16)
    ext_ref[0:POOL_HALO, :] = ext_ref[tm:tm + POOL_HALO, :]


def _l0_in(x, cpat, spat, norm, w_in, q_norm, w_q, kv_norm, w_kv, pool_w, pool_scale, *, scale):
    b, s, d = x.shape
    tm = _tile(s, 512)
    hw = MLA_HEADS * LANES
    vw = MLA_HEADS * MLA_V
    const = lambda shape: pl.BlockSpec(shape, lambda i, j: (0,) * len(shape))
    tok = lambda width: pl.BlockSpec((1, tm, width), lambda i, j: (i, j, 0))
    return pl.pallas_call(
        functools.partial(_l0_in_kernel, tm=tm, scale=scale),
        grid=(b, s // tm),
        in_specs=[tok(d), tok(LANES), tok(LANES), const(norm.shape), const(w_in.shape), const(q_norm.shape),
                  const(w_q.shape), const(kv_norm.shape), const(w_kv.shape), const(pool_w.shape),
                  const(pool_scale.shape)],
        out_specs=[tok(hw), tok(hw), tok(vw), tok(vw), tok(POOL_WIDTH)],
        out_shape=[jax.ShapeDtypeStruct((b, s, hw), BF16), jax.ShapeDtypeStruct((b, s, hw), BF16),
                   jax.ShapeDtypeStruct((b, s, vw), BF16), jax.ShapeDtypeStruct((b, s, vw), BF16),
                   jax.ShapeDtypeStruct((b, s, POOL_WIDTH), BF16)],
        scratch_shapes=[pltpu.VMEM((tm + POOL_HALO, POOL_WIDTH), F32)],
        compiler_params=_params(("parallel", "arbitrary")),
    )(x, cpat, spat, norm, w_in, q_norm, w_q, kv_norm, w_kv, pool_w, pool_scale)


def _attn_kernel(q_ref, k_ref, v_ref, o_ref, *, tq):
    qi = pl.program_id(2)
    r = lax.broadcasted_iota(jnp.int32, (tq, tq), 0)
    c = lax.broadcasted_iota(jnp.int32, (tq, tq), 1)
    causal = c <= r
    heads = []
    for hh in range(2):
        q = q_ref[0, :, hh * LANES:(hh + 1) * LANES]

        def block(off, carry, masked):
            m, l, acc = carry
            k = k_ref[0, pl.ds(off, tq), hh * LANES:(hh + 1) * LANES]
            v = v_ref[0, pl.ds(off, tq), :]
            s = _dot_nt(q, k)
            if masked:
                s = jnp.where(causal, s, NEG_BIG)
            m_new = jnp.maximum(m, jnp.max(s, axis=-1, keepdims=True))
            alpha = jnp.exp(m - m_new)
            p = jnp.exp(s - m_new)
            l = alpha * l + jnp.sum(p, axis=-1, keepdims=True)
            acc = alpha * acc + _dot(p.astype(BF16), v)
            return m_new, l, acc

        init = (jnp.full((tq, 1), NEG_BIG, F32), jnp.zeros((tq, 1), F32), jnp.zeros((tq, LANES), F32))
        carry = lax.fori_loop(0, qi, lambda j, cr: block(pl.multiple_of(j * tq, tq), cr, False), init)
        m, l, acc = block(pl.multiple_of(qi * tq, tq), carry, True)
        heads.append(acc / l)
    lane = lax.broadcasted_iota(jnp.int32, (tq, LANES), 1)
    o_ref[0] = jnp.where(lane < MLA_V, heads[0], heads[1]).astype(BF16)


def _attention(q, k, v):
    b, s, hw = q.shape
    tq = _tile(s, 512)
    pairs = MLA_HEADS // 2
    return pl.pallas_call(
        functools.partial(_attn_kernel, tq=tq),
        grid=(b, pairs, s // tq),
        in_specs=[pl.BlockSpec((1, tq, 2 * LANES), lambda i, p, j: (i, j, p)),
                  pl.BlockSpec((1, s, 2 * LANES), lambda i, p, j: (i, 0, p)),
                  pl.BlockSpec((1, s, LANES), lambda i, p, j: (i, 0, p))],
        out_specs=pl.BlockSpec((1, tq, LANES), lambda i, p, j: (i, j, p)),
        out_shape=jax.ShapeDtypeStruct((b, s, MLA_HEADS * MLA_V), BF16),
        compiler_params=_params(("parallel", "parallel", "arbitrary")),
    )(q, k, v)


def _mid_kernel(x_ref, ya_ref, gz_ref, yb_ref, wo_ref, nrm_ref, win_ref, cw_ref, alog_ref, dtb_ref,
                h_out, q_out, k_out, v_out, gzc_out, gb_out, ext_ref, *, tm):
    t = pl.program_id(1)
    va = MLA_HEADS * MLA_V
    ya = (ya_ref[0].astype(F32) * gz_ref[0].astype(F32)).astype(BF16)
    h1 = x_ref[0] + _dot(ya, wo_ref[0:va, :]) + _dot(yb_ref[0], wo_ref[va:, :])
    h_out[0] = h1

    proj = _dot(_rms(h1, nrm_ref[...]).astype(BF16), win_ref[...])

    @pl.when(t == 0)
    def _():
        ext_ref[0:CONV_HALO, :] = jnp.zeros((CONV_HALO, GDN_CONV_CH), F32)

    xc = proj[:, :GDN_CONV_CH]
    ext_ref[CONV_HALO:CONV_HALO + tm, :] = xc
    conv = xc * cw_ref[CONV_WIDTH - 1:CONV_WIDTH, :]
    for j in range(CONV_WIDTH - 1):
        back = CONV_WIDTH - 1 - j
        conv = conv + ext_ref[CONV_HALO - back:CONV_HALO - back + tm, :] * cw_ref[j:j + 1, :]
    ext_ref[0:CONV_HALO, :] = ext_ref[tm:tm + CONV_HALO, :]
    qkv = _silu(conv)

    def l2(blk):
        return blk * lax.rsqrt(jnp.sum(blk * blk, axis=-1, keepdims=True) + RMS_EPS)

    for h in range(GDN_HEADS):
        lo = h * GDN_DK
        q_out[0, :, lo:lo + GDN_DK] = (l2(qkv[:, lo:lo + GDN_DK]) * GDN_DK ** -0.5).astype(BF16)
        k_out[0, :, lo:lo + GDN_DK] = l2(qkv[:, GDN_QK + lo:GDN_QK + lo + GDN_DK]).astype(BF16)
    v_out[0] = qkv[:, 2 * GDN_QK:].astype(BF16)

    o1 = GDN_CONV_CH + GDN_VW
    gzc_out[0] = _silu(proj[:, GDN_CONV_CH:o1]).astype(BF16)

    ab = proj[:, o1:]
    sp_in = ab + dtb_ref[...]
    softplus = jnp.maximum(sp_in, 0.0) + jnp.log1p(jnp.exp(-jnp.abs(sp_in)))
    g = -jnp.exp(alog_ref[...]) * softplus
    beta = 1.0 / (1.0 + jnp.exp(-ab))
    lane = lax.broadcasted_iota(jnp.int32, (tm, LANES), 1)
    gb_out[0] = jnp.where(lane < GDN_HEADS, g, beta)


def _mid(x, ya, gz, yb, w_out, norm, w_in, conv_w, a_log, dt_bias):
    b, s, d = x.shape
    tm = _tile(s, 256)
    const = lambda shape: pl.BlockSpec(shape, lambda i, j: (0,) * len(shape))
    tok = lambda width: pl.BlockSpec((1, tm, width), lambda i, j: (i, j, 0))
    va = MLA_HEADS * MLA_V
    return pl.pallas_call(
        functools.partial(_mid_kernel, tm=tm),
        grid=(b, s // tm),
        in_specs=[tok(d), tok(va), tok(va), tok(POOL_WIDTH), const(w_out.shape), const(norm.shape),
                  const(w_in.shape), const(conv_w.shape), const(a_log.shape), const(dt_bias.shape)],
        out_specs=[tok(d), tok(GDN_QK), tok(GDN_QK), tok(GDN_VW), tok(GDN_VW), tok(LANES)],
        out_shape=[jax.ShapeDtypeStruct((b, s, d), F32), jax.ShapeDtypeStruct((b, s, GDN_QK), BF16),
                   jax.ShapeDtypeStruct((b, s, GDN_QK), BF16), jax.ShapeDtypeStruct((b, s, GDN_VW), BF16),
                   jax.ShapeDtypeStruct((b, s, GDN_VW), BF16), jax.ShapeDtypeStruct((b, s, LANES), F32)],
        scratch_shapes=[pltpu.VMEM((tm + CONV_HALO, GDN_CONV_CH), F32)],
        compiler_params=_params(("parallel", "arbitrary")),
    )(x, ya, gz, yb, w_out, norm, w_in, conv_w, a_log, dt_bias)


def _unit_lower_inverse(m):
    c = m.shape[0]
    r = lax.broadcasted_iota(jnp.int32, (c, c), 0)
    col = lax.broadcasted_iota(jnp.int32, (c, c), 1)
    t = jnp.where(r == col, 1.0, 0.0) - m
    p = m
    span = 2
    while span < c:
        p = _dot_f32(p, p)
        t = t + _dot_f32(t, p)
        span *= 2
    return t


def _gdn_intra_kernel(q_ref, k_ref, v_ref, gb_ref, u_out, w_out, qd_out, kd_out, at_out, gl_out, *, tt):
    h = pl.program_id(2)
    c = GDN_CHUNK
    r = lax.broadcasted_iota(jnp.int32, (c, c), 0)
    col = lax.broadcasted_iota(jnp.int32, (c, c), 1)
    incl = r >= col
    strict = r > col
    ltri = jnp.where(incl, 1.0, 0.0)
    lane = lax.broadcasted_iota(jnp.int32, (c, LANES), 1)
    sel_g = lane == h
    sel_b = lane == h + GDN_HEADS
    pick = jnp.where(sel_g[0:8], 1.0, 0.0)
    for ci in range(tt // c):
        sl = slice(ci * c, (ci + 1) * c)
        gb = gb_ref[0, sl, :]
        gc_all = _dot_f32(ltri, gb)
        gc = jnp.sum(jnp.where(sel_g, gc_all, 0.0), axis=-1, keepdims=True)
        beta = jnp.sum(jnp.where(sel_b, gb, 0.0), axis=-1, keepdims=True)
        gc_row = lax.dot_general(pick, gc_all, (((1,), (1,)), ((), ())), preferred_element_type=F32,
                                 precision=lax.Precision.HIGHEST)[0:1, :]
        gamma = jnp.exp(jnp.where(incl, gc - gc_row, -jnp.inf))
        gc_last = gc[c - 1:c, :]
        e_col = jnp.exp(gc)

        k16 = k_ref[0, sl, :]
        q16 = q_ref[0, sl, :]
        kf = k16.astype(F32)
        kb = kf * beta
        kk = _dot_nt(kb.astype(BF16), k16)
        tinv = _unit_lower_inverse(jnp.where(strict, kk * gamma, 0.0))
        rhs = jnp.concatenate([(v_ref[0, sl, :].astype(F32) * beta).astype(BF16),
                               (kb * e_col).astype(BF16)], axis=1)
        uw = _dot(tinv.astype(BF16), rhs)
        u_out[0, sl, :] = uw[:, :GDN_DV]
        w_out[0, sl, :] = uw[:, GDN_DV:].astype(BF16)
        at_out[0, 0, sl, :] = (_dot_nt(q16, k16) * gamma).astype(BF16)
        qd_out[0, sl, :] = (q16.astype(F32) * e_col).astype(BF16)
        kd_out[0, sl, :] = (kf * jnp.exp(gc_last - gc)).astype(BF16)
        gl_out[0, 0, ci * 8:(ci + 1) * 8, :] = jnp.broadcast_to(jnp.exp(gc_last), (8, LANES))


def _gdn_intra(q, k, v, gb):
    b, s, _ = q.shape
    tt = _tile(s, 256)
    nch = tt // GDN_CHUNK
    head = pl.BlockSpec((1, tt, LANES), lambda i, j, h: (i, j, h))
    return pl.pallas_call(
        functools.partial(_gdn_intra_kernel, tt=tt),
        grid=(b, s // tt, GDN_HEADS),
        in_specs=[head, head, head, pl.BlockSpec((1, tt, LANES), lambda i, j, h: (i, j, 0))],
        out_specs=[head, head, head, head,
                   pl.BlockSpec((1, 1, tt, GDN_CHUNK), lambda i, j, h: (i, h, j, 0)),
                   pl.BlockSpec((1, 1, nch * 8, LANES), lambda i, j, h: (i, h, j, 0))],
        out_shape=[jax.ShapeDtypeStruct((b, s, GDN_VW), F32), jax.ShapeDtypeStruct((b, s, GDN_QK), BF16),
                   jax.ShapeDtypeStruct((b, s, GDN_QK), BF16), jax.ShapeDtypeStruct((b, s, GDN_QK), BF16),
                   jax.ShapeDtypeStruct((b, GDN_HEADS, s, GDN_CHUNK), BF16),
                   jax.ShapeDtypeStruct((b, GDN_HEADS, s // GDN_CHUNK * 8, LANES), F32)],
        compiler_params=_params(("parallel", "parallel", "parallel")),
    )(q, k, v, gb)


def _gdn_rec_kernel(u_ref, w_ref, qd_ref, kd_ref, at_ref, gl_ref, gz_ref, on_ref, o_out, state_ref, *, tt):
    c = GDN_CHUNK

    @pl.when(pl.program_id(1) == 0)
    def _():
        state_ref[...] = jnp.zeros(state_ref.shape, F32)

    def chunk(ci, _):
        off = pl.multiple_of(ci * c, c)
        off8 = pl.multiple_of(ci * 8, 8)
        for h in range(GDN_HEADS):
            cs = slice(h * LANES, (h + 1) * LANES)
            st = state_ref[h]
            st16 = st.astype(BF16)
            v_new = u_ref[0, pl.ds(off, c), cs] - _dot(w_ref[0, pl.ds(off, c), cs], st16)
            vn16 = v_new.astype(BF16)
            o = _dot(qd_ref[0, pl.ds(off, c), cs], st16) + _dot(at_ref[0, h, pl.ds(off, c), :], vn16)
            gl = gl_ref[0, h, pl.ds(off8, 8), :][0:1, :]
            state_ref[h] = st * gl + _dot_tn(kd_ref[0, pl.ds(off, c), cs], vn16)
            y = _rms(o, on_ref[...]) * gz_ref[0, pl.ds(off, c), cs].astype(F32)
            o_out[0, pl.ds(off, c), cs] = y.astype(BF16)
        return 0

    lax.fori_loop(0, tt // c, chunk, 0)


def _gdn_rec(u, w, qd, kd, at, gl, gz, o_norm):
    b, s, _ = u.shape
    tt = _tile(s, 512)
    nch = tt // GDN_CHUNK
    tok = pl.BlockSpec((1, tt, GDN_VW), lambda i, j: (i, j, 0))
    return pl.pallas_call(
        functools.partial(_gdn_rec_kernel, tt=tt),
        grid=(b, s // tt),
        in_specs=[tok, tok, tok, tok,
                  pl.BlockSpec((1, GDN_HEADS, tt, GDN_CHUNK), lambda i, j: (i, 0, j, 0)),
                  pl.BlockSpec((1, GDN_HEADS, nch * 8, LANES), lambda i, j: (i, 0, j, 0)),
                  tok, pl.BlockSpec((1, GDN_DV), lambda i, j: (0, 0))],
        out_specs=tok,
        out_shape=jax.ShapeDtypeStruct((b, s, GDN_VW), BF16),
        scratch_shapes=[pltpu.VMEM((GDN_HEADS, GDN_DK, GDN_DV), F32)],
        compiler_params=_params(("parallel", "arbitrary")),
    )(u, w, qd, kd, at, gl, gz, o_norm)


def _out_kernel(h_ref, o_ref, wo_ref, nrm_ref, y_out):
    y_out[0] = _rms(h_ref[0] + _dot(o_ref[0], wo_ref[...]), nrm_ref[...])


def _out(h, o, w_out, norm):
    b, s, d = h.shape
    tm = _tile(s, 512)
    tok = lambda width: pl.BlockSpec((1, tm, width), lambda i, j: (i, j, 0))
    const = lambda shape: pl.BlockSpec(shape, lambda i, j: (0,) * len(shape))
    return pl.pallas_call(
        _out_kernel,
        grid=(b, s // tm),
        in_specs=[tok(d), tok(GDN_VW), const(w_out.shape), const(norm.shape)],
        out_specs=tok(d),
        out_shape=jax.ShapeDtypeStruct((b, s, d), F32),
        compiler_params=_params(("parallel", "parallel")),
    )(h, o, w_out, norm)


def _pad_cols(w, width):
    return jnp.pad(w, ((0, 0), (0, width - w.shape[1])))


def _rope_block(r, swapped):
    r1, r2 = r[:, :ROPE_HALF], r[:, ROPE_HALF:]
    z = jnp.zeros((r.shape[0], MLA_NOPE), r.dtype)
    body = jnp.concatenate([z, r2, r1] if swapped else [z, r1, r2], axis=1)
    return _pad_cols(body, LANES)


def _l0_weights(w_in, w_q_b, w_kv_b):
    o1 = MLA_Q_RANK
    o2 = o1 + MLA_KV_RANK
    o3 = o2 + MLA_ROPE
    kr = w_in[:, o2:o3]
    w_in_p = jnp.concatenate([w_in[:, :o2], _rope_block(kr, False), _rope_block(kr, True), w_in[:, o3:]], axis=1)
    qh = w_q_b.reshape(MLA_Q_RANK, MLA_HEADS, MLA_NOPE + MLA_ROPE)
    plain, swapped = [], []
    for h in range(MLA_HEADS):
        nope, rope = qh[:, h, :MLA_NOPE], qh[:, h, MLA_NOPE:]
        plain.append(_rope_block(rope, False).at[:, :MLA_NOPE].set(nope))
        swapped.append(_rope_block(rope, True))
    w_q_p = jnp.concatenate(plain + swapped, axis=1)
    kvh = w_kv_b.reshape(MLA_KV_RANK, MLA_HEADS, MLA_NOPE + MLA_V)
    k_cols = [_pad_cols(kvh[:, h, :MLA_NOPE], LANES) for h in range(MLA_HEADS)]
    v_cols = [kvh[:, h, MLA_NOPE:] for h in range(MLA_HEADS)]
    w_kv_p = jnp.concatenate(k_cols + v_cols, axis=1)
    return w_in_p.astype(BF16), w_q_p.astype(BF16), w_kv_p.astype(BF16)


def kernel(x, positions, norm_ab, w_in_ab, q_a_norm, w_q_b, kv_a_norm, w_kv_b, pool_w, pool_scale, w_out_ab,
           norm_c, w_in_c, conv_w, a_log, dt_bias, o_norm, w_out_c, final_norm):
    assert norm_ab.shape[0] == 1 and norm_c.shape[0] == 1, "one even and one odd layer"
    b, s, d = x.shape

    cos, sin = _rope_tables(positions)
    ones = jnp.ones((b, s, MLA_NOPE), F32)
    tail = jnp.zeros((b, s, LANES - MLA_NOPE - MLA_ROPE), F32)
    cpat = jnp.concatenate([ones, cos, cos, tail], axis=-1)
    spat = jnp.concatenate([jnp.zeros_like(ones), sin, sin, tail], axis=-1)

    w_in_p, w_q_p, w_kv_p = _l0_weights(w_in_ab[0], w_q_b[0], w_kv_b[0])
    q, k, v, gz, yb = _l0_in(
        x, cpat, spat, norm_ab, w_in_p, q_a_norm, w_q_p, kv_a_norm, w_kv_p, pool_w[0].astype(BF16), pool_scale,
        scale=(MLA_NOPE + MLA_ROPE) ** -0.5)
    ya = _attention(q, k, v)

    h1, gq, gk, gv, gzc, gb = _mid(
        x, ya, gz, yb, w_out_ab[0].astype(BF16), norm_c, _pad_cols(w_in_c[0], GDN_CONV_CH + GDN_VW + LANES).astype(BF16),
        conv_w[0], _pad_cols(a_log, LANES), _pad_cols(dt_bias, LANES))
    u, w, qd, kd, at, gl = _gdn_intra(gq, gk, gv, gb)
    o = _gdn_rec(u, w, qd, kd, at, gl, gzc, o_norm)
    return _out(h1, o, w_out_c[0].astype(BF16), final_norm)
```

```python
import functools

import jax
import jax.numpy as jnp
from jax import lax
from jax.experimental import pallas as pl
from jax.experimental.pallas import tpu as pltpu

F32 = jnp.float32
BF16 = jnp.bfloat16

LANES = 128
RMS_EPS = 1e-6

MLA_HEADS = 8
MLA_Q_RANK = 256
MLA_KV_RANK = 128
MLA_NOPE = 64
MLA_ROPE = 32
MLA_V = 64
ROPE_THETA = 10000.0
ROPE_HALF = MLA_ROPE // 2

POOL_WINDOWS = (2, 4, 8, 16)
POOL_GROUP = 128
POOL_WIDTH = POOL_GROUP * len(POOL_WINDOWS)
POOL_HALO = 16

GDN_HEADS = 8
GDN_DK = 128
GDN_DV = 128
CONV_WIDTH = 4
CONV_HALO = 8
GDN_CHUNK = 256
INV_BASE = 8
GDN_QK = GDN_HEADS * GDN_DK
GDN_VW = GDN_HEADS * GDN_DV
GDN_CONV_CH = 2 * GDN_QK + GDN_VW

VMEM_LIMIT = 48 * 1024 * 1024
NEG_BIG = -1e30


def _tile(n, pref):
    t = min(n, pref)
    assert n % t == 0, (n, t)
    return t


def _params(sem):
    return pltpu.CompilerParams(dimension_semantics=sem, vmem_limit_bytes=VMEM_LIMIT)


def _silu(z):
    return z / (1.0 + jnp.exp(-z))


def _rms(x, gain):
    ms = jnp.mean(x * x, axis=-1, keepdims=True)
    return x * lax.rsqrt(ms + RMS_EPS) * gain


def _dot(a, b):
    return jnp.dot(a, b, preferred_element_type=F32)


def _dot_nt(a, b):
    return lax.dot_general(a, b, (((1,), (1,)), ((), ())), preferred_element_type=F32)


def _dot_tn(a, b):
    return lax.dot_general(a, b, (((0,), (0,)), ((), ())), preferred_element_type=F32)


def _dot_f32(a, b):
    return jnp.dot(a, b, preferred_element_type=F32, precision=lax.Precision.HIGHEST)


def _rope_table_kernel(pos_ref, freq_ref, cos_ref, sin_ref):
    ang = pos_ref[...].astype(F32) * freq_ref[...]
    cos_ref[...] = jnp.cos(ang)
    sin_ref[...] = jnp.sin(ang)


def _rope_tables(positions):
    b, s = positions.shape
    n = b * s * ROPE_HALF
    rows = n // LANES
    pos_rep = jnp.repeat(positions.reshape(-1), ROPE_HALF).reshape(rows, LANES)
    inv_freq = 1.0 / (ROPE_THETA ** (jnp.arange(ROPE_HALF, dtype=F32) / ROPE_HALF))
    freq_row = jnp.tile(inv_freq, LANES // ROPE_HALF).reshape(1, LANES)
    tr = _tile(rows, 512)
    cos, sin = pl.pallas_call(
        _rope_table_kernel,
        grid=(rows // tr,),
        in_specs=[pl.BlockSpec((tr, LANES), lambda i: (i, 0)),
                  pl.BlockSpec((1, LANES), lambda i: (0, 0))],
        out_specs=[pl.BlockSpec((tr, LANES), lambda i: (i, 0))] * 2,
        out_shape=[jax.ShapeDtypeStruct((rows, LANES), F32)] * 2,
        compiler_params=_params(("parallel",)),
    )(pos_rep, freq_row)
    return cos.reshape(b, s, ROPE_HALF), sin.reshape(b, s, ROPE_HALF)


def _l0_in_kernel(x_ref, cp_ref, sp_ref, nrm_ref, win_ref, qn_ref, wq_ref, kvn_ref, wkv_ref, pw_ref, ps_ref,
                  q_out, k_out, v_out, gz_out, yb_out, ext_ref, *, tm, scale):
    t = pl.program_id(1)
    hn = _rms(x_ref[0], nrm_ref[...]).astype(BF16)
    proj = _dot(hn, win_ref[...])
    o1 = MLA_Q_RANK
    o2 = o1 + MLA_KV_RANK
    o3 = o2 + LANES
    o4 = o3 + LANES
    o5 = o4 + POOL_WIDTH
    q_lat, kv_lat = proj[:, :o1], proj[:, o1:o2]
    kr, kr_sw = proj[:, o2:o3], proj[:, o3:o4]
    xp = proj[:, o4:o5]
    z = proj[:, o5:]

    lane = lax.broadcasted_iota(jnp.int32, (tm, LANES), 1)
    cpat = cp_ref[0]
    spat = jnp.where(lane < MLA_NOPE + ROPE_HALF, -sp_ref[0], sp_ref[0])

    qq = _dot(_rms(q_lat, qn_ref[...]).astype(BF16), wq_ref[...])
    hw = MLA_HEADS * LANES
    cq, sq = cpat * scale, spat * scale
    for h in range(MLA_HEADS):
        lo = h * LANES
        qh = qq[:, lo:lo + LANES] * cq + qq[:, hw + lo:hw + lo + LANES] * sq
        q_out[0, :, lo:lo + LANES] = qh.astype(BF16)

    kk = _dot(_rms(kv_lat, kvn_ref[...]).astype(BF16), wkv_ref[...])
    k_rope = kr * cpat + kr_sw * spat
    for h in range(MLA_HEADS):
        lo = h * LANES
        k_out[0, :, lo:lo + LANES] = (kk[:, lo:lo + LANES] + k_rope).astype(BF16)
    v_out[0] = kk[:, hw:].astype(BF16)

    gz_out[0] = _silu(z[:, :MLA_HEADS * MLA_V]).astype(BF16)

    @pl.when(t == 0)
    def _():
        ext_ref[0:POOL_HALO, :] = jnp.zeros((POOL_HALO, POOL_WIDTH), F32)

    ext_ref[POOL_HALO:POOL_HALO + tm, :] = xp
    row = lax.broadcasted_iota(jnp.int32, (tm, 1), 0) + t * tm
    zb = z[:, MLA_HEADS * MLA_V:]
    for g, w in enumerate(POOL_WINDOWS):
        lo = g * POOL_GROUP
        xg = xp[:, lo:lo + POOL_GROUP]
        acc = xg
        for j in range(1, w):
            acc = acc + ext_ref[POOL_HALO - j:POOL_HALO - j + tm, lo:lo + POOL_GROUP]
        cnt = jnp.minimum(row + 1, w).astype(F32)
        d = acc / cnt - xg
        y = _dot(d.astype(BF16), pw_ref[g]) * ps_ref[:, lo:lo + POOL_GROUP]
        yb_out[0, :, lo:lo + POOL_GROUP] = (y * _silu(zb[:, lo:lo + POOL_GROUP])).astype(BF16)
    ext_ref[0:POOL_HALO, :] = ext_ref[tm:tm + POOL_HALO, :]


def _l0_in(x, cpat, spat, norm, w_in, q_norm, w_q, kv_norm, w_kv, pool_w, pool_scale, *, scale):
    b, s, d = x.shape
    tm = _tile(s, 512)
    hw = MLA_HEADS * LANES
    vw = MLA_HEADS * MLA_V
    const = lambda shape: pl.BlockSpec(shape, lambda i, j: (0,) * len(shape))
    tok = lambda width: pl.BlockSpec((1, tm, width), lambda i, j: (i, j, 0))
    return pl.pallas_call(
        functools.partial(_l0_in_kernel, tm=tm, scale=scale),
        grid=(b, s // tm),
        in_specs=[tok(d), tok(LANES), tok(LANES), const(norm.shape), const(w_in.shape), const(q_norm.shape),
                  const(w_q.shape), const(kv_norm.shape), const(w_kv.shape), const(pool_w.shape),
                  const(pool_scale.shape)],
        out_specs=[tok(hw), tok(hw), tok(vw), tok(vw), tok(POOL_WIDTH)],
        out_shape=[jax.ShapeDtypeStruct((b, s, hw), BF16), jax.ShapeDtypeStruct((b, s, hw), BF16),
                   jax.ShapeDtypeStruct((b, s, vw), BF16), jax.ShapeDtypeStruct((b, s, vw), BF16),
                   jax.ShapeDtypeStruct((b, s, POOL_WIDTH), BF16)],
        scratch_shapes=[pltpu.VMEM((tm + POOL_HALO, POOL_WIDTH), F32)],
        compiler_params=_params(("parallel", "arbitrary")),
    )(x, cpat, spat, norm, w_in, q_norm, w_q, kv_norm, w_kv, pool_w, pool_scale)


def _attn_kernel(q_ref, k_ref, v_ref, o_ref, *, tq):
    qi = pl.program_id(2)
    r = lax.broadcasted_iota(jnp.int32, (tq, tq), 0)
    c = lax.broadcasted_iota(jnp.int32, (tq, tq), 1)
    causal = c <= r
    heads = []
    for hh in range(2):
        q = q_ref[0, :, hh * LANES:(hh + 1) * LANES]

        def block(off, carry, masked):
            m, l, acc = carry
            k = k_ref[0, pl.ds(off, tq), hh * LANES:(hh + 1) * LANES]
            v = v_ref[0, pl.ds(off, tq), :]
            s = _dot_nt(q, k)
            if masked:
                s = jnp.where(causal, s, NEG_BIG)
            m_new = jnp.maximum(m, jnp.max(s, axis=-1, keepdims=True))
            alpha = jnp.exp(m - m_new)
            p = jnp.exp(s - m_new)
            l = alpha * l + jnp.sum(p, axis=-1, keepdims=True)
            acc = alpha * acc + _dot(p.astype(BF16), v)
            return m_new, l, acc

        init = (jnp.full((tq, 1), NEG_BIG, F32), jnp.zeros((tq, 1), F32), jnp.zeros((tq, LANES), F32))
        carry = lax.fori_loop(0, qi, lambda j, cr: block(pl.multiple_of(j * tq, tq), cr, False), init)
        m, l, acc = block(pl.multiple_of(qi * tq, tq), carry, True)
        heads.append(acc / l)
    lane = lax.broadcasted_iota(jnp.int32, (tq, LANES), 1)
    o_ref[0] = jnp.where(lane < MLA_V, heads[0], heads[1]).astype(BF16)


def _attention(q, k, v):
    b, s, hw = q.shape
    tq = _tile(s, 512)
    pairs = MLA_HEADS // 2
    return pl.pallas_call(
        functools.partial(_attn_kernel, tq=tq),
        grid=(b, pairs, s // tq),
        in_specs=[pl.BlockSpec((1, tq, 2 * LANES), lambda i, p, j: (i, j, p)),
                  pl.BlockSpec((1, s, 2 * LANES), lambda i, p, j: (i, 0, p)),
                  pl.BlockSpec((1, s, LANES), lambda i, p, j: (i, 0, p))],
        out_specs=pl.BlockSpec((1, tq, LANES), lambda i, p, j: (i, j, p)),
        out_shape=jax.ShapeDtypeStruct((b, s, MLA_HEADS * MLA_V), BF16),
        compiler_params=_params(("parallel", "parallel", "arbitrary")),
    )(q, k, v)


def _mid_kernel(x_ref, ya_ref, gz_ref, yb_ref, wo_ref, nrm_ref, win_ref, cw_ref, alog_ref, dtb_ref,
                h_out, q_out, k_out, v_out, gzc_out, gb_out, ext_ref, *, tm):
    t = pl.program_id(1)
    va = MLA_HEADS * MLA_V
    ya = (ya_ref[0].astype(F32) * gz_ref[0].astype(F32)).astype(BF16)
    h1 = x_ref[0] + _dot(ya, wo_ref[0:va, :]) + _dot(yb_ref[0], wo_ref[va:, :])
    h_out[0] = h1

    proj = _dot(_rms(h1, nrm_ref[...]).astype(BF16), win_ref[...])

    @pl.when(t == 0)
    def _():
        ext_ref[0:CONV_HALO, :] = jnp.zeros((CONV_HALO, GDN_CONV_CH), F32)

    xc = proj[:, :GDN_CONV_CH]
    ext_ref[CONV_HALO:CONV_HALO + tm, :] = xc
    conv = xc * cw_ref[CONV_WIDTH - 1:CONV_WIDTH, :]
    for j in range(CONV_WIDTH - 1):
        back = CONV_WIDTH - 1 - j
        conv = conv + ext_ref[CONV_HALO - back:CONV_HALO - back + tm, :] * cw_ref[j:j + 1, :]
    ext_ref[0:CONV_HALO, :] = ext_ref[tm:tm + CONV_HALO, :]
    qkv = _silu(conv)

    def l2(blk):
        return blk * lax.rsqrt(jnp.sum(blk * blk, axis=-1, keepdims=True) + RMS_EPS)

    for h in range(GDN_HEADS):
        lo = h * GDN_DK
        q_out[0, :, lo:lo + GDN_DK] = (l2(qkv[:, lo:lo + GDN_DK]) * GDN_DK ** -0.5).astype(BF16)
        k_out[0, :, lo:lo + GDN_DK] = l2(qkv[:, GDN_QK + lo:GDN_QK + lo + GDN_DK]).astype(BF16)
    v_out[0] = qkv[:, 2 * GDN_QK:].astype(BF16)

    o1 = GDN_CONV_CH + GDN_VW
    gzc_out[0] = _silu(proj[:, GDN_CONV_CH:o1]).astype(BF16)

    ab = proj[:, o1:]
    sp_in = ab + dtb_ref[...]
    softplus = jnp.maximum(sp_in, 0.0) + jnp.log1p(jnp.exp(-jnp.abs(sp_in)))
    g = -jnp.exp(alog_ref[...]) * softplus
    beta = 1.0 / (1.0 + jnp.exp(-ab))
    lane = lax.broadcasted_iota(jnp.int32, (tm, LANES), 1)
    gb_out[0] = jnp.where(lane < GDN_HEADS, g, beta)


def _mid(x, ya, gz, yb, w_out, norm, w_in, conv_w, a_log, dt_bias):
    b, s, d = x.shape
    tm = _tile(s, 256)
    const = lambda shape: pl.BlockSpec(shape, lambda i, j: (0,) * len(shape))
    tok = lambda width: pl.BlockSpec((1, tm, width), lambda i, j: (i, j, 0))
    va = MLA_HEADS * MLA_V
    return pl.pallas_call(
        functools.partial(_mid_kernel, tm=tm),
        grid=(b, s // tm),
        in_specs=[tok(d), tok(va), tok(va), tok(POOL_WIDTH), const(w_out.shape), const(norm.shape),
                  const(w_in.shape), const(conv_w.shape), const(a_log.shape), const(dt_bias.shape)],
        out_specs=[tok(d), tok(GDN_QK), tok(GDN_QK), tok(GDN_VW), tok(GDN_VW), tok(LANES)],
        out_shape=[jax.ShapeDtypeStruct((b, s, d), F32), jax.ShapeDtypeStruct((b, s, GDN_QK), BF16),
                   jax.ShapeDtypeStruct((b, s, GDN_QK), BF16), jax.ShapeDtypeStruct((b, s, GDN_VW), BF16),
                   jax.ShapeDtypeStruct((b, s, GDN_VW), BF16), jax.ShapeDtypeStruct((b, s, LANES), F32)],
        scratch_shapes=[pltpu.VMEM((tm + CONV_HALO, GDN_CONV_CH), F32)],
        compiler_params=_params(("parallel", "arbitrary")),
    )(x, ya, gz, yb, w_out, norm, w_in, conv_w, a_log, dt_bias)


def _chunk_cumsum(g):
    n = g.shape[0]
    row = lax.broadcasted_iota(jnp.int32, g.shape, 0)
    sh = 1
    while sh < n:
        g = g + jnp.where(row >= sh, pltpu.roll(g, sh, axis=0), 0.0)
        sh *= 2
    return g


def _inverse_masks(c):
    idx = jnp.arange(c)
    blk = lambda size: (idx[:, None] // size) == (idx[None, :] // size)
    masks = [blk(INV_BASE)]
    size = INV_BASE
    while size < c:
        masks.append(blk(2 * size) & ~blk(size) & (idx[:, None] > idx[None, :]))
        size *= 2
    return jnp.stack(masks).astype(F32)


def _gdn_intra_kernel(q_ref, k_ref, v_ref, gb_ref, msk_ref, u_out, w_out, qd_out, kd_out, at_out, gl_out,
                      m_ref, t_ref, t16_ref, x_ref, *, c):
    heads = range(GDN_HEADS)
    hcols = lambda h: slice(h * LANES, (h + 1) * LANES)
    r = lax.broadcasted_iota(jnp.int32, (c, c), 0)
    col = lax.broadcasted_iota(jnp.int32, (c, c), 1)

    gb = gb_ref[0]
    gc_all = _chunk_cumsum(gb)
    gc_t = gc_all.T
    gc_last = gc_all[c - 1:c, :]
    e_all = jnp.exp(gc_all)
    ek_all = jnp.exp(gc_last - gc_all)
    gl_out[0] = jnp.broadcast_to(jnp.exp(gc_last), (8, LANES))

    for h in heads:
        gamma = jnp.exp(jnp.where(r >= col, gc_all[:, h:h + 1] - gc_t[h:h + 1, :], -jnp.inf))
        k16 = k_ref[0, :, hcols(h)]
        q16 = q_ref[0, :, hcols(h)]
        kf = k16.astype(F32)
        kb16 = (kf * gb[:, GDN_HEADS + h:GDN_HEADS + h + 1]).astype(BF16)
        m_ref[h] = jnp.where(r > col, _dot_nt(kb16, k16) * gamma, 0.0)
        at_out[0, h] = (_dot_nt(q16, k16) * gamma).astype(BF16)
        qd_out[0, :, hcols(h)] = (q16.astype(F32) * e_all[:, h:h + 1]).astype(BF16)
        kd_out[0, :, hcols(h)] = (kf * ek_all[:, h:h + 1]).astype(BF16)

    for h in heads:
        p = m_ref[h] * msk_ref[0]
        t_ref[h] = jnp.where(r == col, 1.0, 0.0) - p
        x_ref[h] = p.astype(BF16)
    span = 2
    while span < INV_BASE:
        for h in heads:
            p16 = x_ref[h]
            p = _dot(p16, p16)
            t = t_ref[h]
            t_ref[h] = t + _dot(t.astype(BF16), p.astype(BF16))
            x_ref[h] = p.astype(BF16)
        span *= 2
    for h in heads:
        t16_ref[h] = t_ref[h].astype(BF16)

    level = 1
    size = INV_BASE
    while size < c:
        for h in heads:
            off = (m_ref[h] * msk_ref[level]).astype(BF16)
            x_ref[h] = _dot(off, t16_ref[h]).astype(BF16)
        for h in heads:
            t = t_ref[h] - _dot(t16_ref[h], x_ref[h])
            t_ref[h] = t
            t16_ref[h] = t.astype(BF16)
        size *= 2
        level += 1

    for h in heads:
        kf = k_ref[0, :, hcols(h)].astype(F32)
        beta = gb[:, GDN_HEADS + h:GDN_HEADS + h + 1]
        rhs = jnp.concatenate([(v_ref[0, :, hcols(h)].astype(F32) * beta).astype(BF16),
                               (kf * (beta * e_all[:, h:h + 1])).astype(BF16)], axis=1)
        uw = _dot(t16_ref[h], rhs)
        u_out[0, :, hcols(h)] = uw[:, :GDN_DV]
        w_out[0, :, hcols(h)] = uw[:, GDN_DV:].astype(BF16)


def _gdn_intra(q, k, v, gb):
    b, s, _ = q.shape
    c = _tile(s, GDN_CHUNK)
    masks = _inverse_masks(c)
    tok = pl.BlockSpec((1, c, GDN_VW), lambda i, j: (i, j, 0))
    sq = lambda dt: pltpu.VMEM((GDN_HEADS, c, c), dt)
    return pl.pallas_call(
        functools.partial(_gdn_intra_kernel, c=c),
        grid=(b, s // c),
        in_specs=[tok, tok, tok, pl.BlockSpec((1, c, LANES), lambda i, j: (i, j, 0)),
                  pl.BlockSpec(masks.shape, lambda i, j: (0, 0, 0))],
        out_specs=[tok, tok, tok, tok,
                   pl.BlockSpec((1, GDN_HEADS, c, c), lambda i, j: (i, 0, j, 0)),
                   pl.BlockSpec((1, 8, LANES), lambda i, j: (i, j, 0))],
        out_shape=[jax.ShapeDtypeStruct((b, s, GDN_VW), F32), jax.ShapeDtypeStruct((b, s, GDN_QK), BF16),
                   jax.ShapeDtypeStruct((b, s, GDN_QK), BF16), jax.ShapeDtypeStruct((b, s, GDN_QK), BF16),
                   jax.ShapeDtypeStruct((b, GDN_HEADS, s, c), BF16),
                   jax.ShapeDtypeStruct((b, s // c * 8, LANES), F32)],
        scratch_shapes=[sq(F32), sq(F32), sq(BF16), sq(BF16)],
        compiler_params=_params(("parallel", "parallel")),
    )(q, k, v, gb, masks)


def _gdn_rec_kernel(u_ref, w_ref, qd_ref, kd_ref, at_ref, gl_ref, gz_ref, on_ref, o_out, state_ref):
    heads = range(GDN_HEADS)
    hcols = lambda h: slice(h * LANES, (h + 1) * LANES)

    @pl.when(pl.program_id(1) == 0)
    def _():
        state_ref[...] = jnp.zeros(state_ref.shape, F32)

    st16 = [state_ref[h].astype(BF16) for h in heads]
    ws = [_dot(w_ref[0, :, hcols(h)], st16[h]) for h in heads]
    qs = [_dot(qd_ref[0, :, hcols(h)], st16[h]) for h in heads]
    vn16 = [(u_ref[0, :, hcols(h)] - ws[h]).astype(BF16) for h in heads]
    gl = gl_ref[0]
    for h in heads:
        o = qs[h] + _dot(at_ref[0, h], vn16[h])
        state_ref[h] = state_ref[h] * gl[0:1, h:h + 1] + _dot_tn(kd_ref[0, :, hcols(h)], vn16[h])
        y = _rms(o, on_ref[...]) * gz_ref[0, :, hcols(h)].astype(F32)
        o_out[0, :, hcols(h)] = y.astype(BF16)


def _gdn_rec(u, w, qd, kd, at, gl, gz, o_norm):
    b, s, _ = u.shape
    c = at.shape[-1]
    tok = pl.BlockSpec((1, c, GDN_VW), lambda i, j: (i, j, 0))
    return pl.pallas_call(
        _gdn_rec_kernel,
        grid=(b, s // c),
        in_specs=[tok, tok, tok, tok,
                  pl.BlockSpec((1, GDN_HEADS, c, c), lambda i, j: (i, 0, j, 0)),
                  pl.BlockSpec((1, 8, LANES), lambda i, j: (i, j, 0)),
                  tok, pl.BlockSpec((1, GDN_DV), lambda i, j: (0, 0))],
        out_specs=tok,
        out_shape=jax.ShapeDtypeStruct((b, s, GDN_VW), BF16),
        scratch_shapes=[pltpu.VMEM((GDN_HEADS, GDN_DK, GDN_DV), F32)],
        compiler_params=_params(("parallel", "arbitrary")),
    )(u, w, qd, kd, at, gl, gz, o_norm)


def _out_kernel(h_ref, o_ref, wo_ref, nrm_ref, y_out):
    y_out[0] = _rms(h_ref[0] + _dot(o_ref[0], wo_ref[...]), nrm_ref[...])


def _out(h, o, w_out, norm):
    b, s, d = h.shape
    tm = _tile(s, 512)
    tok = lambda width: pl.BlockSpec((1, tm, width), lambda i, j: (i, j, 0))
    const = lambda shape: pl.BlockSpec(shape, lambda i, j: (0,) * len(shape))
    return pl.pallas_call(
        _out_kernel,
        grid=(b, s // tm),
        in_specs=[tok(d), tok(GDN_VW), const(w_out.shape), const(norm.shape)],
        out_specs=tok(d),
        out_shape=jax.ShapeDtypeStruct((b, s, d), F32),
        compiler_params=_params(("parallel", "parallel")),
    )(h, o, w_out, norm)


def _pad_cols(w, width):
    return jnp.pad(w, ((0, 0), (0, width - w.shape[1])))


def _rope_block(r, swapped):
    r1, r2 = r[:, :ROPE_HALF], r[:, ROPE_HALF:]
    z = jnp.zeros((r.shape[0], MLA_NOPE), r.dtype)
    body = jnp.concatenate([z, r2, r1] if swapped else [z, r1, r2], axis=1)
    return _pad_cols(body, LANES)


def _l0_weights(w_in, w_q_b, w_kv_b):
    o1 = MLA_Q_RANK
    o2 = o1 + MLA_KV_RANK
    o3 = o2 + MLA_ROPE
    kr = w_in[:, o2:o3]
    w_in_p = jnp.concatenate([w_in[:, :o2], _rope_block(kr, False), _rope_block(kr, True), w_in[:, o3:]], axis=1)
    qh = w_q_b.reshape(MLA_Q_RANK, MLA_HEADS, MLA_NOPE + MLA_ROPE)
    plain, swapped = [], []
    for h in range(MLA_HEADS):
        nope, rope = qh[:, h, :MLA_NOPE], qh[:, h, MLA_NOPE:]
        plain.append(_rope_block(rope, False).at[:, :MLA_NOPE].set(nope))
        swapped.append(_rope_block(rope, True))
    w_q_p = jnp.concatenate(plain + swapped, axis=1)
    kvh = w_kv_b.reshape(MLA_KV_RANK, MLA_HEADS, MLA_NOPE + MLA_V)
    k_cols = [_pad_cols(kvh[:, h, :MLA_NOPE], LANES) for h in range(MLA_HEADS)]
    v_cols = [kvh[:, h, MLA_NOPE:] for h in range(MLA_HEADS)]
    w_kv_p = jnp.concatenate(k_cols + v_cols, axis=1)
    return w_in_p.astype(BF16), w_q_p.astype(BF16), w_kv_p.astype(BF16)


def kernel(x, positions, norm_ab, w_in_ab, q_a_norm, w_q_b, kv_a_norm, w_kv_b, pool_w, pool_scale, w_out_ab,
           norm_c, w_in_c, conv_w, a_log, dt_bias, o_norm, w_out_c, final_norm):
    assert norm_ab.shape[0] == 1 and norm_c.shape[0] == 1, "one even and one odd layer"
    b, s, d = x.shape

    cos, sin = _rope_tables(positions)
    ones = jnp.ones((b, s, MLA_NOPE), F32)
    tail = jnp.zeros((b, s, LANES - MLA_NOPE - MLA_ROPE), F32)
    cpat = jnp.concatenate([ones, cos, cos, tail], axis=-1)
    spat = jnp.concatenate([jnp.zeros_like(ones), sin, sin, tail], axis=-1)

    w_in_p, w_q_p, w_kv_p = _l0_weights(w_in_ab[0], w_q_b[0], w_kv_b[0])
    q, k, v, gz, yb = _l0_in(
        x, cpat, spat, norm_ab, w_in_p, q_a_norm, w_q_p, kv_a_norm, w_kv_p, pool_w[0].astype(BF16), pool_scale,
        scale=(MLA_NOPE + MLA_ROPE) ** -0.5)
    ya = _attention(q, k, v)

    h1, gq, gk, gv, gzc, gb = _mid(
        x, ya, gz, yb, w_out_ab[0].astype(BF16), norm_c, _pad_cols(w_in_c[0], GDN_CONV_CH + GDN_VW + LANES).astype(BF16),
        conv_w[0], _pad_cols(a_log, LANES), _pad_cols(dt_bias, LANES))
    u, w, qd, kd, at, gl = _gdn_intra(gq, gk, gv, gb)
    o = _gdn_rec(u, w, qd, kd, at, gl, gzc, o_norm)
    return _out(h1, o, w_out_c[0].astype(BF16), final_norm)
```

```python
import functools
import math

import jax
import jax.numpy as jnp
from jax import lax
from jax.experimental import pallas as pl
from jax.experimental.pallas import tpu as pltpu

F32 = jnp.float32
BF16 = jnp.bfloat16

LANES = 128
RMS_EPS = 1e-6

MLA_HEADS = 8
MLA_Q_RANK = 256
MLA_KV_RANK = 128
MLA_NOPE = 64
MLA_ROPE = 32
MLA_V = 64
ROPE_THETA = 10000.0
ROPE_HALF = MLA_ROPE // 2

POOL_WINDOWS = (2, 4, 8, 16)
POOL_GROUP = 128
POOL_WIDTH = POOL_GROUP * len(POOL_WINDOWS)
POOL_HALO = 16

GDN_HEADS = 8
GDN_DK = 128
GDN_DV = 128
CONV_WIDTH = 4
CONV_HALO = 8
ATTN_GROUP = 4
MID_BLOCK = 512
GDN_CHUNK = 256
INV_BASE = 8
GDN_QK = GDN_HEADS * GDN_DK
GDN_VW = GDN_HEADS * GDN_DV
GDN_CONV_CH = 2 * GDN_QK + GDN_VW

VMEM_LIMIT = 48 * 1024 * 1024
NEG_BIG = -1e30


def _tile(n, pref):
    t = min(n, pref)
    assert n % t == 0, (n, t)
    return t


def _params(sem):
    return pltpu.CompilerParams(dimension_semantics=sem, vmem_limit_bytes=VMEM_LIMIT)


def _silu(z):
    h = 0.5 * z
    return h + h * jnp.tanh(h)


def _rms(x, gain):
    ms = jnp.mean(x * x, axis=-1, keepdims=True)
    return x * lax.rsqrt(ms + RMS_EPS) * gain


def _dot(a, b):
    return jnp.dot(a, b, preferred_element_type=F32)


def _dot_nt(a, b):
    return lax.dot_general(a, b, (((1,), (1,)), ((), ())), preferred_element_type=F32)


def _dot_tn(a, b):
    return lax.dot_general(a, b, (((0,), (0,)), ((), ())), preferred_element_type=F32)


def _dot_f32(a, b):
    return jnp.dot(a, b, preferred_element_type=F32, precision=lax.Precision.HIGHEST)


def _rope_table_kernel(pos_ref, freq_ref, cos_ref, sin_ref):
    ang = pos_ref[...].astype(F32) * freq_ref[...]
    cos_ref[...] = jnp.cos(ang)
    sin_ref[...] = jnp.sin(ang)


def _rope_tables(positions):
    b, s = positions.shape
    n = b * s * ROPE_HALF
    rows = n // LANES
    pos_rep = jnp.repeat(positions.reshape(-1), ROPE_HALF).reshape(rows, LANES)
    inv_freq = 1.0 / (ROPE_THETA ** (jnp.arange(ROPE_HALF, dtype=F32) / ROPE_HALF))
    freq_row = jnp.tile(inv_freq, LANES // ROPE_HALF).reshape(1, LANES)
    tr = _tile(rows, 512)
    cos, sin = pl.pallas_call(
        _rope_table_kernel,
        grid=(rows // tr,),
        in_specs=[pl.BlockSpec((tr, LANES), lambda i: (i, 0)),
                  pl.BlockSpec((1, LANES), lambda i: (0, 0))],
        out_specs=[pl.BlockSpec((tr, LANES), lambda i: (i, 0))] * 2,
        out_shape=[jax.ShapeDtypeStruct((rows, LANES), F32)] * 2,
        compiler_params=_params(("parallel",)),
    )(pos_rep, freq_row)
    return cos.reshape(b, s, ROPE_HALF), sin.reshape(b, s, ROPE_HALF)


def _l0_in_kernel(x_ref, cp_ref, sp_ref, nrm_ref, win_ref, qn_ref, wq_ref, kvn_ref, wkv_ref, pw_ref, ps_ref,
                  q_out, k_out, v_out, gz_out, yb_out, ext_ref, hn_ref, *, tm, scale):
    t = pl.program_id(1)
    hn_ref[...] = _rms(x_ref[0], nrm_ref[...]).astype(BF16)
    o1 = MLA_Q_RANK
    o2 = o1 + MLA_KV_RANK
    o3 = o2 + LANES
    o4 = o3 + LANES
    o5 = o4 + POOL_WIDTH
    o6 = o5 + MLA_HEADS * MLA_V
    hw = MLA_HEADS * LANES
    proj = lambda lo, hi: _dot(hn_ref[...], win_ref[:, lo:hi])

    lat = proj(0, o4)
    xp = proj(o4, o5)
    zb = proj(o6, o6 + POOL_WIDTH)
    qq = _dot(_rms(lat[:, :o1], qn_ref[...]).astype(BF16), wq_ref[...])

    @pl.when(t == 0)
    def _():
        ext_ref[0:POOL_HALO, :] = jnp.zeros((POOL_HALO, POOL_WIDTH), F32)

    ext_ref[POOL_HALO:POOL_HALO + tm, :] = xp
    row = lax.broadcasted_iota(jnp.int32, (tm, 1), 0) + t * tm
    for g, w in enumerate(POOL_WINDOWS):
        lo = g * POOL_GROUP
        acc = ext_ref[:, lo:lo + POOL_GROUP]
        span = 1
        while span < w:
            acc = acc + pltpu.roll(acc, span, axis=0)
            span *= 2
        cnt = jnp.minimum(row + 1, w).astype(F32)
        d = acc[POOL_HALO:, :] / cnt - xp[:, lo:lo + POOL_GROUP]
        y = _dot(d.astype(BF16), pw_ref[g]) * ps_ref[:, lo:lo + POOL_GROUP]
        yb_out[0, :, lo:lo + POOL_GROUP] = (y * _silu(zb[:, lo:lo + POOL_GROUP])).astype(BF16)
    ext_ref[0:POOL_HALO, :] = ext_ref[tm:tm + POOL_HALO, :]

    za = proj(o5, o6)

    lane = lax.broadcasted_iota(jnp.int32, (tm, LANES), 1)
    cpat = cp_ref[0]
    spat = jnp.where(lane < MLA_NOPE + ROPE_HALF, -sp_ref[0], sp_ref[0])
    cq, sq = cpat * scale, spat * scale
    for h in range(MLA_HEADS):
        lo = h * LANES
        qh = qq[:, lo:lo + LANES] * cq + qq[:, hw + lo:hw + lo + LANES] * sq
        q_out[0, :, lo:lo + LANES] = qh.astype(BF16)

    kk = _dot(_rms(lat[:, o1:o2], kvn_ref[...]).astype(BF16), wkv_ref[...])
    gz_out[0] = _silu(za).astype(BF16)
    k_rope = lat[:, o2:o3] * cpat + lat[:, o3:o4] * spat
    for h in range(MLA_HEADS):
        lo = h * LANES
        k_out[0, :, lo:lo + LANES] = (kk[:, lo:lo + LANES] + k_rope).astype(BF16)
    ones_lane = jnp.where(lane == MLA_V, 1.0, 0.0)
    for h in range(MLA_HEADS):
        lo = h * LANES
        v_out[0, :, lo:lo + LANES] = (kk[:, hw + lo:hw + lo + LANES] + ones_lane).astype(BF16)


def _l0_in(x, cpat, spat, norm, w_in, q_norm, w_q, kv_norm, w_kv, pool_w, pool_scale, *, scale):
    b, s, d = x.shape
    tm = _tile(s, 512)
    hw = MLA_HEADS * LANES
    vw = MLA_HEADS * MLA_V
    const = lambda shape: pl.BlockSpec(shape, lambda i, j: (0,) * len(shape))
    tok = lambda width: pl.BlockSpec((1, tm, width), lambda i, j: (i, j, 0))
    return pl.pallas_call(
        functools.partial(_l0_in_kernel, tm=tm, scale=scale),
        grid=(b, s // tm),
        in_specs=[tok(d), tok(LANES), tok(LANES), const(norm.shape), const(w_in.shape), const(q_norm.shape),
                  const(w_q.shape), const(kv_norm.shape), const(w_kv.shape), const(pool_w.shape),
                  const(pool_scale.shape)],
        out_specs=[tok(hw), tok(hw), tok(hw), tok(vw), tok(POOL_WIDTH)],
        out_shape=[jax.ShapeDtypeStruct((b, s, hw), BF16), jax.ShapeDtypeStruct((b, s, hw), BF16),
                   jax.ShapeDtypeStruct((b, s, hw), BF16), jax.ShapeDtypeStruct((b, s, vw), BF16),
                   jax.ShapeDtypeStruct((b, s, POOL_WIDTH), BF16)],
        scratch_shapes=[pltpu.VMEM((tm + POOL_HALO, POOL_WIDTH), F32), pltpu.VMEM((tm, d), BF16)],
        compiler_params=_params(("parallel", "arbitrary")),
    )(x, cpat, spat, norm, w_in, q_norm, w_q, kv_norm, w_kv, pool_w, pool_scale)


def _attn_kernel(q_ref, k_ref, v_ref, o_ref, m_ref, acc_ref, *, tq):
    qi = pl.program_id(2)
    pair = range(ATTN_GROUP)
    hcols = lambda hh: slice(hh * LANES, (hh + 1) * LANES)
    m_ref[...] = jnp.full(m_ref.shape, NEG_BIG, F32)
    acc_ref[...] = jnp.zeros(acc_ref.shape, F32)

    def block(j, masked):
        rows = pl.ds(pl.multiple_of(j * tq, tq), tq)
        s = [_dot_nt(q_ref[0, :, hcols(hh)], k_ref[0, rows, hcols(hh)]) for hh in pair]
        for hh in pair:
            sh = s[hh]
            if masked:
                r = lax.broadcasted_iota(jnp.int32, (tq, tq), 0)
                c = lax.broadcasted_iota(jnp.int32, (tq, tq), 1)
                sh = jnp.where(c <= r, sh, NEG_BIG)
            m_old = m_ref[hh]
            m_new = jnp.maximum(m_old, jnp.max(sh, axis=-1, keepdims=True))
            p = jnp.concatenate([jnp.exp2(sh[:, c0:c0 + LANES] - m_new) for c0 in range(0, tq, LANES)], axis=1)
            acc_ref[hh] = (jnp.exp2(m_old - m_new) * acc_ref[hh]
                           + _dot(p.astype(BF16), v_ref[0, rows, hcols(hh)]))
            m_ref[hh] = m_new

    def body(j, _):
        block(j, False)
        return 0

    lax.fori_loop(0, qi, body, 0)
    block(qi, True)

    lane = lax.broadcasted_iota(jnp.int32, (tq, LANES), 1)
    for hh in range(0, ATTN_GROUP, 2):
        lo, hi = [acc_ref[h] / acc_ref[h][:, MLA_V:MLA_V + 1] for h in (hh, hh + 1)]
        o_ref[0, :, hcols(hh // 2)] = jnp.where(lane < MLA_V, lo, pltpu.roll(hi, MLA_V, axis=1)).astype(BF16)


def _attention(q, k, v):
    b, s, hw = q.shape
    tq = _tile(s, 512)
    gw = ATTN_GROUP * LANES
    whole = pl.BlockSpec((1, s, gw), lambda i, p, j: (i, 0, p), pipeline_mode=pl.Buffered(1))
    return pl.pallas_call(
        functools.partial(_attn_kernel, tq=tq),
        grid=(b, MLA_HEADS // ATTN_GROUP, s // tq),
        in_specs=[pl.BlockSpec((1, tq, gw), lambda i, p, j: (i, j, p)), whole, whole],
        out_specs=pl.BlockSpec((1, tq, ATTN_GROUP * MLA_V), lambda i, p, j: (i, j, p)),
        out_shape=jax.ShapeDtypeStruct((b, s, MLA_HEADS * MLA_V), BF16),
        scratch_shapes=[pltpu.VMEM((ATTN_GROUP, tq, LANES), F32), pltpu.VMEM((ATTN_GROUP, tq, LANES), F32)],
        compiler_params=_params(("parallel", "parallel", "arbitrary")),
    )(q, k, v)


def _mid_kernel(x_ref, ya_ref, gz_ref, yb_ref, wo_ref, nrm_ref, win_ref, cw_ref, alog_ref, dtb_ref,
                h_out, q_out, k_out, v_out, gzc_out, gb_out, ext_ref, hn_ref, *, tm):
    t = pl.program_id(1)
    va = MLA_HEADS * MLA_V
    ya = (ya_ref[0].astype(F32) * gz_ref[0].astype(F32)).astype(BF16)
    h1 = x_ref[0] + _dot(ya, wo_ref[0:va, :]) + _dot(yb_ref[0], wo_ref[va:, :])
    h_out[0] = h1
    hn_ref[...] = _rms(h1, nrm_ref[...]).astype(BF16)

    @pl.when(t == 0)
    def _():
        ext_ref[0:CONV_HALO, :] = jnp.zeros((CONV_HALO, GDN_CONV_CH), F32)

    def l2(blk, scale=1.0):
        return blk * (lax.rsqrt(jnp.sum(blk * blk, axis=-1, keepdims=True) + RMS_EPS) * scale)

    def conv_block(lo, xc):
        cols = slice(lo, lo + MID_BLOCK)
        ext_ref[CONV_HALO:CONV_HALO + tm, cols] = xc
        conv = xc * cw_ref[CONV_WIDTH - 1:CONV_WIDTH, cols]
        for j in range(CONV_WIDTH - 1):
            back = CONV_WIDTH - 1 - j
            conv = conv + ext_ref[CONV_HALO - back:CONV_HALO - back + tm, cols] * cw_ref[j:j + 1, cols]
        ext_ref[0:CONV_HALO, cols] = ext_ref[tm:tm + CONV_HALO, cols]
        act = _silu(conv)
        for i in range(MID_BLOCK // LANES):
            c0 = lo + i * LANES
            blk = act[:, i * LANES:(i + 1) * LANES]
            if c0 < GDN_QK:
                q_out[0, :, c0:c0 + LANES] = l2(blk, GDN_DK ** -0.5).astype(BF16)
            elif c0 < 2 * GDN_QK:
                k_out[0, :, c0 - GDN_QK:c0 - GDN_QK + LANES] = l2(blk).astype(BF16)
            else:
                v_out[0, :, c0 - 2 * GDN_QK:c0 - 2 * GDN_QK + LANES] = blk.astype(BF16)

    def gate_block(lo, z):
        gzc_out[0, :, lo - GDN_CONV_CH:lo - GDN_CONV_CH + MID_BLOCK] = _silu(z).astype(BF16)

    def decay_block(lo, ab):
        sp_in = ab + dtb_ref[...]
        softplus = jnp.maximum(sp_in, 0.0) + jnp.log1p(jnp.exp(-jnp.abs(sp_in)))
        g = -jnp.exp(alog_ref[...]) * softplus
        beta = 1.0 / (1.0 + jnp.exp(-ab))
        lane = lax.broadcasted_iota(jnp.int32, (tm, LANES), 1)
        gb_out[0] = jnp.where(lane < GDN_HEADS, g, beta)

    o1 = GDN_CONV_CH + GDN_VW
    stages = [(lo, MID_BLOCK, conv_block) for lo in range(0, GDN_CONV_CH, MID_BLOCK)]
    stages += [(lo, MID_BLOCK, gate_block) for lo in range(GDN_CONV_CH, o1, MID_BLOCK)]
    stages += [(o1, LANES, decay_block)]
    pending = None
    for lo, width, consume in stages:
        proj = _dot(hn_ref[...], win_ref[:, lo:lo + width])
        if pending is not None:
            pending[0](pending[1], pending[2])
        pending = (consume, lo, proj)
    pending[0](pending[1], pending[2])


def _mid(x, ya, gz, yb, w_out, norm, w_in, conv_w, a_log, dt_bias):
    b, s, d = x.shape
    tm = _tile(s, 512)
    const = lambda shape: pl.BlockSpec(shape, lambda i, j: (0,) * len(shape), pipeline_mode=pl.Buffered(1))
    tok = lambda width: pl.BlockSpec((1, tm, width), lambda i, j: (i, j, 0))
    va = MLA_HEADS * MLA_V
    return pl.pallas_call(
        functools.partial(_mid_kernel, tm=tm),
        grid=(b, s // tm),
        in_specs=[tok(d), tok(va), tok(va), tok(POOL_WIDTH), const(w_out.shape), const(norm.shape),
                  const(w_in.shape), const(conv_w.shape), const(a_log.shape), const(dt_bias.shape)],
        out_specs=[tok(d), tok(GDN_QK), tok(GDN_QK), tok(GDN_VW), tok(GDN_VW), tok(LANES)],
        out_shape=[jax.ShapeDtypeStruct((b, s, d), F32), jax.ShapeDtypeStruct((b, s, GDN_QK), BF16),
                   jax.ShapeDtypeStruct((b, s, GDN_QK), BF16), jax.ShapeDtypeStruct((b, s, GDN_VW), BF16),
                   jax.ShapeDtypeStruct((b, s, GDN_VW), BF16), jax.ShapeDtypeStruct((b, s, LANES), F32)],
        scratch_shapes=[pltpu.VMEM((tm + CONV_HALO, GDN_CONV_CH), F32), pltpu.VMEM((tm, d), BF16)],
        compiler_params=_params(("parallel", "arbitrary")),
    )(x, ya, gz, yb, w_out, norm, w_in, conv_w, a_log, dt_bias)


def _chunk_cumsum(g):
    n = g.shape[0]
    row = lax.broadcasted_iota(jnp.int32, g.shape, 0)
    sh = 1
    while sh < n:
        g = g + jnp.where(row >= sh, pltpu.roll(g, sh, axis=0), 0.0)
        sh *= 2
    return g


def _inverse_masks(c):
    idx = jnp.arange(c)
    blk = lambda size: (idx[:, None] // size) == (idx[None, :] // size)
    masks = [blk(INV_BASE)]
    size = INV_BASE
    while size < c:
        masks.append(blk(2 * size) & ~blk(size) & (idx[:, None] > idx[None, :]))
        size *= 2
    return jnp.stack(masks).astype(F32)


def _gdn_intra_kernel(q_ref, k_ref, v_ref, gb_ref, msk_ref, u_out, w_out, qd_out, kd_out, at_out, gl_out,
                      m_ref, t_ref, t16_ref, x_ref, *, c):
    heads = range(GDN_HEADS)
    hcols = lambda h: slice(h * LANES, (h + 1) * LANES)
    r = lax.broadcasted_iota(jnp.int32, (c, c), 0)
    col = lax.broadcasted_iota(jnp.int32, (c, c), 1)

    gb = gb_ref[0]
    gc_all = _chunk_cumsum(gb)
    gc_t = gc_all.T
    gc_last = gc_all[c - 1:c, :]
    e_all = jnp.exp(gc_all)
    ek_all = jnp.exp(gc_last - gc_all)
    gl_out[0] = jnp.broadcast_to(jnp.exp(gc_last), (8, LANES))

    for h in heads:
        gamma = jnp.exp(jnp.where(r >= col, gc_all[:, h:h + 1] - gc_t[h:h + 1, :], -jnp.inf))
        k16 = k_ref[0, :, hcols(h)]
        q16 = q_ref[0, :, hcols(h)]
        kf = k16.astype(F32)
        kb16 = (kf * gb[:, GDN_HEADS + h:GDN_HEADS + h + 1]).astype(BF16)
        m_ref[h] = jnp.where(r > col, _dot_nt(kb16, k16) * gamma, 0.0)
        at_out[0, h] = (_dot_nt(q16, k16) * gamma).astype(BF16)
        qd_out[0, :, hcols(h)] = (q16.astype(F32) * e_all[:, h:h + 1]).astype(BF16)
        kd_out[0, :, hcols(h)] = (kf * ek_all[:, h:h + 1]).astype(BF16)

    for h in heads:
        p = m_ref[h] * msk_ref[0]
        t_ref[h] = jnp.where(r == col, 1.0, 0.0) - p
        x_ref[h] = p.astype(BF16)
    span = 2
    while span < INV_BASE:
        for h in heads:
            p16 = x_ref[h]
            p = _dot(p16, p16)
            t = t_ref[h]
            t_ref[h] = t + _dot(t.astype(BF16), p.astype(BF16))
            x_ref[h] = p.astype(BF16)
        span *= 2
    for h in heads:
        t16_ref[h] = t_ref[h].astype(BF16)

    level = 1
    size = INV_BASE
    while size < c:
        for h in heads:
            off = (m_ref[h] * msk_ref[level]).astype(BF16)
            x_ref[h] = _dot(off, t16_ref[h]).astype(BF16)
        for h in heads:
            t = t_ref[h] - _dot(t16_ref[h], x_ref[h])
            t_ref[h] = t
            t16_ref[h] = t.astype(BF16)
        size *= 2
        level += 1

    for h in heads:
        kf = k_ref[0, :, hcols(h)].astype(F32)
        beta = gb[:, GDN_HEADS + h:GDN_HEADS + h + 1]
        rhs = jnp.concatenate([(v_ref[0, :, hcols(h)].astype(F32) * beta).astype(BF16),
                               (kf * (beta * e_all[:, h:h + 1])).astype(BF16)], axis=1)
        uw = _dot(t16_ref[h], rhs)
        u_out[0, :, hcols(h)] = uw[:, :GDN_DV]
        w_out[0, :, hcols(h)] = uw[:, GDN_DV:].astype(BF16)


def _gdn_intra(q, k, v, gb):
    b, s, _ = q.shape
    c = _tile(s, GDN_CHUNK)
    masks = _inverse_masks(c)
    tok = pl.BlockSpec((1, c, GDN_VW), lambda i, j: (i, j, 0))
    sq = lambda dt: pltpu.VMEM((GDN_HEADS, c, c), dt)
    return pl.pallas_call(
        functools.partial(_gdn_intra_kernel, c=c),
        grid=(b, s // c),
        in_specs=[tok, tok, tok, pl.BlockSpec((1, c, LANES), lambda i, j: (i, j, 0)),
                  pl.BlockSpec(masks.shape, lambda i, j: (0, 0, 0))],
        out_specs=[tok, tok, tok, tok,
                   pl.BlockSpec((1, GDN_HEADS, c, c), lambda i, j: (i, 0, j, 0)),
                   pl.BlockSpec((1, 8, LANES), lambda i, j: (i, j, 0))],
        out_shape=[jax.ShapeDtypeStruct((b, s, GDN_VW), F32), jax.ShapeDtypeStruct((b, s, GDN_QK), BF16),
                   jax.ShapeDtypeStruct((b, s, GDN_QK), BF16), jax.ShapeDtypeStruct((b, s, GDN_QK), BF16),
                   jax.ShapeDtypeStruct((b, GDN_HEADS, s, c), BF16),
                   jax.ShapeDtypeStruct((b, s // c * 8, LANES), F32)],
        scratch_shapes=[sq(F32), sq(F32), sq(BF16), sq(BF16)],
        compiler_params=_params(("parallel", "parallel")),
    )(q, k, v, gb, masks)


def _gdn_rec_kernel(u_ref, w_ref, qd_ref, kd_ref, at_ref, gl_ref, gz_ref, on_ref, o_out, state_ref):
    heads = range(GDN_HEADS)
    hcols = lambda h: slice(h * LANES, (h + 1) * LANES)

    @pl.when(pl.program_id(1) == 0)
    def _():
        state_ref[...] = jnp.zeros(state_ref.shape, F32)

    st16 = [state_ref[h].astype(BF16) for h in heads]
    ws = [_dot(w_ref[0, :, hcols(h)], st16[h]) for h in heads]
    qs = [_dot(qd_ref[0, :, hcols(h)], st16[h]) for h in heads]
    vn16 = [(u_ref[0, :, hcols(h)] - ws[h]).astype(BF16) for h in heads]
    gl = gl_ref[0]
    for h in heads:
        o = qs[h] + _dot(at_ref[0, h], vn16[h])
        state_ref[h] = state_ref[h] * gl[0:1, h:h + 1] + _dot_tn(kd_ref[0, :, hcols(h)], vn16[h])
        y = _rms(o, on_ref[...]) * gz_ref[0, :, hcols(h)].astype(F32)
        o_out[0, :, hcols(h)] = y.astype(BF16)


def _gdn_rec(u, w, qd, kd, at, gl, gz, o_norm):
    b, s, _ = u.shape
    c = at.shape[-1]
    tok = pl.BlockSpec((1, c, GDN_VW), lambda i, j: (i, j, 0))
    return pl.pallas_call(
        _gdn_rec_kernel,
        grid=(b, s // c),
        in_specs=[tok, tok, tok, tok,
                  pl.BlockSpec((1, GDN_HEADS, c, c), lambda i, j: (i, 0, j, 0)),
                  pl.BlockSpec((1, 8, LANES), lambda i, j: (i, j, 0)),
                  tok, pl.BlockSpec((1, GDN_DV), lambda i, j: (0, 0))],
        out_specs=tok,
        out_shape=jax.ShapeDtypeStruct((b, s, GDN_VW), BF16),
        scratch_shapes=[pltpu.VMEM((GDN_HEADS, GDN_DK, GDN_DV), F32)],
        compiler_params=_params(("parallel", "arbitrary")),
    )(u, w, qd, kd, at, gl, gz, o_norm)


def _out_kernel(h_ref, o_ref, wo_ref, nrm_ref, y_out):
    y_out[0] = _rms(h_ref[0] + _dot(o_ref[0], wo_ref[...]), nrm_ref[...])


def _out(h, o, w_out, norm):
    b, s, d = h.shape
    tm = _tile(s, 512)
    tok = lambda width: pl.BlockSpec((1, tm, width), lambda i, j: (i, j, 0))
    const = lambda shape: pl.BlockSpec(shape, lambda i, j: (0,) * len(shape))
    return pl.pallas_call(
        _out_kernel,
        grid=(b, s // tm),
        in_specs=[tok(d), tok(GDN_VW), const(w_out.shape), const(norm.shape)],
        out_specs=tok(d),
        out_shape=jax.ShapeDtypeStruct((b, s, d), F32),
        compiler_params=_params(("parallel", "parallel")),
    )(h, o, w_out, norm)


def _pad_cols(w, width):
    return jnp.pad(w, ((0, 0), (0, width - w.shape[1])))


def _rope_block(r, swapped):
    r1, r2 = r[:, :ROPE_HALF], r[:, ROPE_HALF:]
    z = jnp.zeros((r.shape[0], MLA_NOPE), r.dtype)
    body = jnp.concatenate([z, r2, r1] if swapped else [z, r1, r2], axis=1)
    return _pad_cols(body, LANES)


def _l0_weights(w_in, w_q_b, w_kv_b):
    o1 = MLA_Q_RANK
    o2 = o1 + MLA_KV_RANK
    o3 = o2 + MLA_ROPE
    kr = w_in[:, o2:o3]
    w_in_p = jnp.concatenate([w_in[:, :o2], _rope_block(kr, False), _rope_block(kr, True), w_in[:, o3:]], axis=1)
    qh = w_q_b.reshape(MLA_Q_RANK, MLA_HEADS, MLA_NOPE + MLA_ROPE)
    plain, swapped = [], []
    for h in range(MLA_HEADS):
        nope, rope = qh[:, h, :MLA_NOPE], qh[:, h, MLA_NOPE:]
        plain.append(_rope_block(rope, False).at[:, :MLA_NOPE].set(nope))
        swapped.append(_rope_block(rope, True))
    w_q_p = jnp.concatenate(plain + swapped, axis=1)
    kvh = w_kv_b.reshape(MLA_KV_RANK, MLA_HEADS, MLA_NOPE + MLA_V)
    k_cols = [_pad_cols(kvh[:, h, :MLA_NOPE], LANES) for h in range(MLA_HEADS)]
    v_cols = [_pad_cols(kvh[:, h, MLA_NOPE:], LANES) for h in range(MLA_HEADS)]
    w_kv_p = jnp.concatenate(k_cols + v_cols, axis=1)
    return w_in_p.astype(BF16), w_q_p.astype(BF16), w_kv_p.astype(BF16)


def kernel(x, positions, norm_ab, w_in_ab, q_a_norm, w_q_b, kv_a_norm, w_kv_b, pool_w, pool_scale, w_out_ab,
           norm_c, w_in_c, conv_w, a_log, dt_bias, o_norm, w_out_c, final_norm):
    assert norm_ab.shape[0] == 1 and norm_c.shape[0] == 1, "one even and one odd layer"
    b, s, d = x.shape

    cos, sin = _rope_tables(positions)
    ones = jnp.ones((b, s, MLA_NOPE), F32)
    tail = jnp.zeros((b, s, LANES - MLA_NOPE - MLA_ROPE), F32)
    cpat = jnp.concatenate([ones, cos, cos, tail], axis=-1)
    spat = jnp.concatenate([jnp.zeros_like(ones), sin, sin, tail], axis=-1)

    w_in_p, w_q_p, w_kv_p = _l0_weights(w_in_ab[0], w_q_b[0], w_kv_b[0])
    q, k, v, gz, yb = _l0_in(
        x, cpat, spat, norm_ab, w_in_p, q_a_norm, w_q_p, kv_a_norm, w_kv_p, pool_w[0].astype(BF16), pool_scale,
        scale=(MLA_NOPE + MLA_ROPE) ** -0.5 * math.log2(math.e))
    ya = _attention(q, k, v)

    h1, gq, gk, gv, gzc, gb = _mid(
        x, ya, gz, yb, w_out_ab[0].astype(BF16), norm_c, _pad_cols(w_in_c[0], GDN_CONV_CH + GDN_VW + LANES).astype(BF16),
        conv_w[0], _pad_cols(a_log, LANES), _pad_cols(dt_bias, LANES))
    u, w, qd, kd, at, gl = _gdn_intra(gq, gk, gv, gb)
    o = _gdn_rec(u, w, qd, kd, at, gl, gzc, o_norm)
    return _out(h1, o, w_out_c[0].astype(BF16), final_norm)
```

```python
import functools
import math

import jax
import jax.numpy as jnp
from jax import lax
from jax.experimental import pallas as pl
from jax.experimental.pallas import tpu as pltpu

F32 = jnp.float32
BF16 = jnp.bfloat16

LANES = 128
RMS_EPS = 1e-6

MLA_HEADS = 8
MLA_Q_RANK = 256
MLA_KV_RANK = 128
MLA_NOPE = 64
MLA_ROPE = 32
MLA_V = 64
ROPE_THETA = 10000.0
ROPE_HALF = MLA_ROPE // 2

POOL_WINDOWS = (2, 4, 8, 16)
POOL_GROUP = 128
POOL_WIDTH = POOL_GROUP * len(POOL_WINDOWS)
POOL_HALO = 16

GDN_HEADS = 8
GDN_DK = 128
GDN_DV = 128
CONV_WIDTH = 4
CONV_HALO = 8
ATTN_GROUP = 4
MID_BLOCK = 512
GDN_CHUNK = 256
INV_BASE = 8
SOLVE_BLOCK = 64
GDN_QK = GDN_HEADS * GDN_DK
GDN_VW = GDN_HEADS * GDN_DV
GDN_CONV_CH = 2 * GDN_QK + GDN_VW

VMEM_LIMIT = 48 * 1024 * 1024
NEG_BIG = -1e30


def _tile(n, pref):
    t = min(n, pref)
    assert n % t == 0, (n, t)
    return t


def _params(sem):
    return pltpu.CompilerParams(dimension_semantics=sem, vmem_limit_bytes=VMEM_LIMIT)


def _silu(z):
    h = 0.5 * z
    return h + h * jnp.tanh(h)


def _rms(x, gain):
    ms = jnp.mean(x * x, axis=-1, keepdims=True)
    return x * lax.rsqrt(ms + RMS_EPS) * gain


def _dot(a, b):
    return jnp.dot(a, b, preferred_element_type=F32)


def _dot_nt(a, b):
    return lax.dot_general(a, b, (((1,), (1,)), ((), ())), preferred_element_type=F32)


def _dot_tn(a, b):
    return lax.dot_general(a, b, (((0,), (0,)), ((), ())), preferred_element_type=F32)


def _dot_f32(a, b):
    return jnp.dot(a, b, preferred_element_type=F32, precision=lax.Precision.HIGHEST)


def _rope_table_kernel(pos_ref, freq_ref, cos_ref, sin_ref):
    ang = pos_ref[...].astype(F32) * freq_ref[...]
    cos_ref[...] = jnp.cos(ang)
    sin_ref[...] = jnp.sin(ang)


def _rope_tables(positions):
    b, s = positions.shape
    n = b * s * ROPE_HALF
    rows = n // LANES
    pos_rep = jnp.repeat(positions.reshape(-1), ROPE_HALF).reshape(rows, LANES)
    inv_freq = 1.0 / (ROPE_THETA ** (jnp.arange(ROPE_HALF, dtype=F32) / ROPE_HALF))
    freq_row = jnp.tile(inv_freq, LANES // ROPE_HALF).reshape(1, LANES)
    tr = _tile(rows, 512)
    cos, sin = pl.pallas_call(
        _rope_table_kernel,
        grid=(rows // tr,),
        in_specs=[pl.BlockSpec((tr, LANES), lambda i: (i, 0)),
                  pl.BlockSpec((1, LANES), lambda i: (0, 0))],
        out_specs=[pl.BlockSpec((tr, LANES), lambda i: (i, 0))] * 2,
        out_shape=[jax.ShapeDtypeStruct((rows, LANES), F32)] * 2,
        compiler_params=_params(("parallel",)),
    )(pos_rep, freq_row)
    return cos.reshape(b, s, ROPE_HALF), sin.reshape(b, s, ROPE_HALF)


def _l0_in_kernel(x_ref, cp_ref, sp_ref, nrm_ref, win_ref, qn_ref, wq_ref, kvn_ref, wkv_ref, pw_ref, ps_ref,
                  q_out, k_out, v_out, gz_out, yb_out, ext_ref, hn_ref, *, tm, scale):
    t = pl.program_id(1)
    hn_ref[...] = _rms(x_ref[0], nrm_ref[...]).astype(BF16)
    o1 = MLA_Q_RANK
    o2 = o1 + MLA_KV_RANK
    o3 = o2 + LANES
    o4 = o3 + LANES
    o5 = o4 + POOL_WIDTH
    o6 = o5 + MLA_HEADS * MLA_V
    hw = MLA_HEADS * LANES
    proj = lambda lo, hi: _dot(hn_ref[...], win_ref[:, lo:hi])

    lat = proj(0, o4)
    xp = proj(o4, o5)
    zb = proj(o6, o6 + POOL_WIDTH)
    qq = _dot(_rms(lat[:, :o1], qn_ref[...]).astype(BF16), wq_ref[...])

    @pl.when(t == 0)
    def _():
        ext_ref[0:POOL_HALO, :] = jnp.zeros((POOL_HALO, POOL_WIDTH), F32)

    ext_ref[POOL_HALO:POOL_HALO + tm, :] = xp
    row = lax.broadcasted_iota(jnp.int32, (tm, 1), 0) + t * tm
    for g, w in enumerate(POOL_WINDOWS):
        lo = g * POOL_GROUP
        acc = ext_ref[:, lo:lo + POOL_GROUP]
        span = 1
        while span < w:
            acc = acc + pltpu.roll(acc, span, axis=0)
            span *= 2
        cnt = jnp.minimum(row + 1, w).astype(F32)
        d = acc[POOL_HALO:, :] / cnt - xp[:, lo:lo + POOL_GROUP]
        y = _dot(d.astype(BF16), pw_ref[g]) * ps_ref[:, lo:lo + POOL_GROUP]
        yb_out[0, :, lo:lo + POOL_GROUP] = (y * _silu(zb[:, lo:lo + POOL_GROUP])).astype(BF16)
    ext_ref[0:POOL_HALO, :] = ext_ref[tm:tm + POOL_HALO, :]

    za = proj(o5, o6)

    lane = lax.broadcasted_iota(jnp.int32, (tm, LANES), 1)
    cpat = cp_ref[0]
    spat = jnp.where(lane < MLA_NOPE + ROPE_HALF, -sp_ref[0], sp_ref[0])
    cq, sq = cpat * scale, spat * scale
    for h in range(MLA_HEADS):
        lo = h * LANES
        qh = qq[:, lo:lo + LANES] * cq + qq[:, hw + lo:hw + lo + LANES] * sq
        q_out[0, :, lo:lo + LANES] = qh.astype(BF16)

    kk = _dot(_rms(lat[:, o1:o2], kvn_ref[...]).astype(BF16), wkv_ref[...])
    gz_out[0] = _silu(za).astype(BF16)
    k_rope = lat[:, o2:o3] * cpat + lat[:, o3:o4] * spat
    for h in range(MLA_HEADS):
        lo = h * LANES
        k_out[0, :, lo:lo + LANES] = (kk[:, lo:lo + LANES] + k_rope).astype(BF16)
    ones_lane = jnp.where(lane == MLA_V, 1.0, 0.0)
    for h in range(MLA_HEADS):
        lo = h * LANES
        v_out[0, :, lo:lo + LANES] = (kk[:, hw + lo:hw + lo + LANES] + ones_lane).astype(BF16)


def _l0_in(x, cpat, spat, norm, w_in, q_norm, w_q, kv_norm, w_kv, pool_w, pool_scale, *, scale):
    b, s, d = x.shape
    tm = _tile(s, 512)
    hw = MLA_HEADS * LANES
    vw = MLA_HEADS * MLA_V
    const = lambda shape: pl.BlockSpec(shape, lambda i, j: (0,) * len(shape))
    tok = lambda width: pl.BlockSpec((1, tm, width), lambda i, j: (i, j, 0))
    return pl.pallas_call(
        functools.partial(_l0_in_kernel, tm=tm, scale=scale),
        grid=(b, s // tm),
        in_specs=[tok(d), tok(LANES), tok(LANES), const(norm.shape), const(w_in.shape), const(q_norm.shape),
                  const(w_q.shape), const(kv_norm.shape), const(w_kv.shape), const(pool_w.shape),
                  const(pool_scale.shape)],
        out_specs=[tok(hw), tok(hw), tok(hw), tok(vw), tok(POOL_WIDTH)],
        out_shape=[jax.ShapeDtypeStruct((b, s, hw), BF16), jax.ShapeDtypeStruct((b, s, hw), BF16),
                   jax.ShapeDtypeStruct((b, s, hw), BF16), jax.ShapeDtypeStruct((b, s, vw), BF16),
                   jax.ShapeDtypeStruct((b, s, POOL_WIDTH), BF16)],
        scratch_shapes=[pltpu.VMEM((tm + POOL_HALO, POOL_WIDTH), F32), pltpu.VMEM((tm, d), BF16)],
        compiler_params=_params(("parallel", "arbitrary")),
    )(x, cpat, spat, norm, w_in, q_norm, w_q, kv_norm, w_kv, pool_w, pool_scale)


def _attn_kernel(q_ref, k_ref, v_ref, o_ref, m_ref, acc_ref, *, tq):
    qi = pl.program_id(2)
    pair = range(ATTN_GROUP)
    hcols = lambda hh: slice(hh * LANES, (hh + 1) * LANES)
    m_ref[...] = jnp.full(m_ref.shape, NEG_BIG, F32)
    acc_ref[...] = jnp.zeros(acc_ref.shape, F32)

    def blocks(items):
        rows = [pl.ds(pl.multiple_of(j * tq, tq), tq) for j, _ in items]
        s = [[_dot_nt(q_ref[0, :, hcols(hh)], k_ref[0, rw, hcols(hh)]) for hh in pair] for rw in rows]
        for bi, (_, masked) in enumerate(items):
            for hh in pair:
                sh = s[bi][hh]
                if masked:
                    r = lax.broadcasted_iota(jnp.int32, (tq, tq), 0)
                    c = lax.broadcasted_iota(jnp.int32, (tq, tq), 1)
                    sh = jnp.where(c <= r, sh, NEG_BIG)
                m_old = m_ref[hh]
                m_new = jnp.maximum(m_old, jnp.max(sh, axis=-1, keepdims=True))
                p = jnp.concatenate([jnp.exp2(sh[:, c0:c0 + LANES] - m_new) for c0 in range(0, tq, LANES)],
                                    axis=1)
                acc_ref[hh] = (jnp.exp2(m_old - m_new) * acc_ref[hh]
                               + _dot(p.astype(BF16), v_ref[0, rows[bi], hcols(hh)]))
                m_ref[hh] = m_new

    def body(jj, _):
        blocks([(2 * jj, False), (2 * jj + 1, False)])
        return 0

    lax.fori_loop(0, qi // 2, body, 0)

    @pl.when(qi % 2 == 1)
    def _():
        blocks([(qi - 1, False), (qi, True)])

    @pl.when(qi % 2 == 0)
    def _():
        blocks([(qi, True)])

    lane = lax.broadcasted_iota(jnp.int32, (tq, LANES), 1)
    for hh in range(0, ATTN_GROUP, 2):
        lo, hi = [acc_ref[h] / acc_ref[h][:, MLA_V:MLA_V + 1] for h in (hh, hh + 1)]
        o_ref[0, :, hcols(hh // 2)] = jnp.where(lane < MLA_V, lo, pltpu.roll(hi, MLA_V, axis=1)).astype(BF16)


def _attention(q, k, v):
    b, s, hw = q.shape
    tq = _tile(s, 512)
    gw = ATTN_GROUP * LANES
    whole = pl.BlockSpec((1, s, gw), lambda i, p, j: (i, 0, p), pipeline_mode=pl.Buffered(1))
    return pl.pallas_call(
        functools.partial(_attn_kernel, tq=tq),
        grid=(b, MLA_HEADS // ATTN_GROUP, s // tq),
        in_specs=[pl.BlockSpec((1, tq, gw), lambda i, p, j: (i, j, p)), whole, whole],
        out_specs=pl.BlockSpec((1, tq, ATTN_GROUP * MLA_V), lambda i, p, j: (i, j, p)),
        out_shape=jax.ShapeDtypeStruct((b, s, MLA_HEADS * MLA_V), BF16),
        scratch_shapes=[pltpu.VMEM((ATTN_GROUP, tq, LANES), F32), pltpu.VMEM((ATTN_GROUP, tq, LANES), F32)],
        compiler_params=_params(("parallel", "parallel", "arbitrary")),
    )(q, k, v)


def _mid_kernel(x_ref, ya_ref, gz_ref, yb_ref, wo_ref, nrm_ref, win_ref, cw_ref, alog_ref, dtb_ref,
                h_out, q_out, k_out, v_out, gzc_out, gb_out, ext_ref, hn_ref, *, tm):
    t = pl.program_id(1)
    va = MLA_HEADS * MLA_V
    ya = (ya_ref[0].astype(F32) * gz_ref[0].astype(F32)).astype(BF16)
    h1 = x_ref[0] + _dot(ya, wo_ref[0:va, :]) + _dot(yb_ref[0], wo_ref[va:, :])
    h_out[0] = h1
    hn_ref[...] = _rms(h1, nrm_ref[...]).astype(BF16)

    @pl.when(t == 0)
    def _():
        ext_ref[0:CONV_HALO, :] = jnp.zeros((CONV_HALO, GDN_CONV_CH), F32)

    def l2(blk, scale=1.0):
        return blk * (lax.rsqrt(jnp.sum(blk * blk, axis=-1, keepdims=True) + RMS_EPS) * scale)

    def conv_block(lo, xc):
        cols = slice(lo, lo + MID_BLOCK)
        ext_ref[CONV_HALO:CONV_HALO + tm, cols] = xc
        xe = ext_ref[:, cols]
        conv = xc * cw_ref[CONV_WIDTH - 1:CONV_WIDTH, cols]
        for j in range(CONV_WIDTH - 1):
            back = CONV_WIDTH - 1 - j
            conv = conv + pltpu.roll(xe, back, axis=0)[CONV_HALO:, :] * cw_ref[j:j + 1, cols]
        ext_ref[0:CONV_HALO, cols] = ext_ref[tm:tm + CONV_HALO, cols]
        act = _silu(conv)
        for i in range(MID_BLOCK // LANES):
            c0 = lo + i * LANES
            blk = act[:, i * LANES:(i + 1) * LANES]
            if c0 < GDN_QK:
                q_out[0, :, c0:c0 + LANES] = l2(blk, GDN_DK ** -0.5).astype(BF16)
            elif c0 < 2 * GDN_QK:
                k_out[0, :, c0 - GDN_QK:c0 - GDN_QK + LANES] = l2(blk).astype(BF16)
            else:
                v_out[0, :, c0 - 2 * GDN_QK:c0 - 2 * GDN_QK + LANES] = blk.astype(BF16)

    def gate_block(lo, z):
        gzc_out[0, :, lo - GDN_CONV_CH:lo - GDN_CONV_CH + MID_BLOCK] = _silu(z).astype(BF16)

    def decay_block(lo, ab):
        sp_in = ab + dtb_ref[...]
        softplus = jnp.maximum(sp_in, 0.0) + jnp.log1p(jnp.exp(-jnp.abs(sp_in)))
        g = -jnp.exp(alog_ref[...]) * softplus
        beta = 1.0 / (1.0 + jnp.exp(-ab))
        lane = lax.broadcasted_iota(jnp.int32, (tm, LANES), 1)
        gb_out[0] = jnp.where(lane < GDN_HEADS, g, beta)

    o1 = GDN_CONV_CH + GDN_VW
    stages = [(lo, MID_BLOCK, conv_block) for lo in range(0, GDN_CONV_CH, MID_BLOCK)]
    stages += [(lo, MID_BLOCK, gate_block) for lo in range(GDN_CONV_CH, o1, MID_BLOCK)]
    stages += [(o1, LANES, decay_block)]
    pending = None
    for lo, width, consume in stages:
        proj = _dot(hn_ref[...], win_ref[:, lo:lo + width])
        if pending is not None:
            pending[0](pending[1], pending[2])
        pending = (consume, lo, proj)
    pending[0](pending[1], pending[2])


def _mid(x, ya, gz, yb, w_out, norm, w_in, conv_w, a_log, dt_bias):
    b, s, d = x.shape
    tm = _tile(s, 512)
    const = lambda shape: pl.BlockSpec(shape, lambda i, j: (0,) * len(shape), pipeline_mode=pl.Buffered(1))
    tok = lambda width: pl.BlockSpec((1, tm, width), lambda i, j: (i, j, 0))
    va = MLA_HEADS * MLA_V
    return pl.pallas_call(
        functools.partial(_mid_kernel, tm=tm),
        grid=(b, s // tm),
        in_specs=[tok(d), tok(va), tok(va), tok(POOL_WIDTH), const(w_out.shape), const(norm.shape),
                  const(w_in.shape), const(conv_w.shape), const(a_log.shape), const(dt_bias.shape)],
        out_specs=[tok(d), tok(GDN_QK), tok(GDN_QK), tok(GDN_VW), tok(GDN_VW), tok(LANES)],
        out_shape=[jax.ShapeDtypeStruct((b, s, d), F32), jax.ShapeDtypeStruct((b, s, GDN_QK), BF16),
                   jax.ShapeDtypeStruct((b, s, GDN_QK), BF16), jax.ShapeDtypeStruct((b, s, GDN_VW), BF16),
                   jax.ShapeDtypeStruct((b, s, GDN_VW), BF16), jax.ShapeDtypeStruct((b, s, LANES), F32)],
        scratch_shapes=[pltpu.VMEM((tm + CONV_HALO, GDN_CONV_CH), F32), pltpu.VMEM((tm, d), BF16)],
        compiler_params=_params(("parallel", "arbitrary")),
    )(x, ya, gz, yb, w_out, norm, w_in, conv_w, a_log, dt_bias)


def _chunk_cumsum(g):
    n = g.shape[0]
    row = lax.broadcasted_iota(jnp.int32, g.shape, 0)
    sh = 1
    while sh < n:
        g = g + jnp.where(row >= sh, pltpu.roll(g, sh, axis=0), 0.0)
        sh *= 2
    return g


def _inverse_masks(c):
    idx = jnp.arange(c)
    blk = lambda size: (idx[:, None] // size) == (idx[None, :] // size)
    masks = [blk(INV_BASE)]
    size = INV_BASE
    while size < SOLVE_BLOCK:
        masks.append(blk(2 * size) & ~blk(size) & (idx[:, None] > idx[None, :]))
        size *= 2
    return jnp.stack(masks).astype(F32)


def _gdn_kernel(q_ref, k_ref, v_ref, gb_ref, gz_ref, h_ref, msk_ref, on_ref, wo_ref, fn_ref, y_out,
                m_ref, m16_ref, t_ref, t16_ref, x_ref, at_ref, rhs_ref, vn_ref, og_ref, state_ref, *, c):
    heads = range(GDN_HEADS)
    hcols = lambda h: slice(h * LANES, (h + 1) * LANES)
    r = lax.broadcasted_iota(jnp.int32, (c, c), 0)
    col = lax.broadcasted_iota(jnp.int32, (c, c), 1)

    @pl.when(pl.program_id(1) == 0)
    def _():
        state_ref[...] = jnp.zeros(state_ref.shape, F32)

    gb = gb_ref[0]
    gc_all = _chunk_cumsum(gb)
    gc_t = gc_all.T
    gc_last = gc_all[c - 1:c, :]
    e_all = jnp.exp(gc_all)
    ek_all = jnp.exp(gc_last - gc_all)
    g_last = jnp.exp(gc_last)
    beta = lambda h: gb[:, GDN_HEADS + h:GDN_HEADS + h + 1]

    for h in heads:
        gamma = jnp.exp(jnp.where(r >= col, gc_all[:, h:h + 1] - gc_t[h:h + 1, :], -jnp.inf))
        k16 = k_ref[0, :, hcols(h)]
        kb16 = (k16.astype(F32) * beta(h)).astype(BF16)
        m = jnp.where(r > col, _dot_nt(kb16, k16) * gamma, 0.0)
        m_ref[h] = m
        m16_ref[h] = m.astype(BF16)
        at_ref[h] = (_dot_nt(q_ref[0, :, hcols(h)], k16) * gamma).astype(BF16)

    for h in heads:
        p = m_ref[h] * msk_ref[0]
        t_ref[h] = jnp.where(r == col, 1.0, 0.0) - p
        x_ref[h] = p.astype(BF16)
    span = 2
    while span < INV_BASE:
        for h in heads:
            p16 = x_ref[h]
            p = _dot(p16, p16)
            t = t_ref[h]
            t_ref[h] = t + _dot(t.astype(BF16), p.astype(BF16))
            x_ref[h] = p.astype(BF16)
        span *= 2
    for h in heads:
        t16_ref[h] = t_ref[h].astype(BF16)

    level = 1
    size = INV_BASE
    while size < SOLVE_BLOCK:
        for h in heads:
            off = (m_ref[h] * msk_ref[level]).astype(BF16)
            x_ref[h] = _dot(off, t16_ref[h]).astype(BF16)
        for h in heads:
            t = t_ref[h] - _dot(t16_ref[h], x_ref[h])
            t_ref[h] = t
            t16_ref[h] = t.astype(BF16)
        size *= 2
        level += 1

    st16 = [state_ref[h].astype(BF16) for h in heads]
    qs = []
    for h in heads:
        kf = k_ref[0, :, hcols(h)].astype(F32)
        kbe16 = (kf * (beta(h) * e_all[:, h:h + 1])).astype(BF16)
        rhs = v_ref[0, :, hcols(h)].astype(F32) * beta(h) - _dot(kbe16, st16[h])
        rhs_ref[h] = rhs
        x_ref[h, :, 0:GDN_DV] = rhs.astype(BF16)
        vn_ref[h] = jnp.zeros((c, GDN_DV), BF16)
        qd16 = (q_ref[0, :, hcols(h)].astype(F32) * e_all[:, h:h + 1]).astype(BF16)
        qs.append(_dot(qd16, st16[h]))

    for i in range(c // SOLVE_BLOCK):
        rows = slice(i * SOLVE_BLOCK, (i + 1) * SOLVE_BLOCK)
        if i > 0:
            for h in heads:
                rhs = rhs_ref[h, rows, :] - _dot(m16_ref[h, rows, :], vn_ref[h])
                x_ref[h, rows, 0:GDN_DV] = rhs.astype(BF16)
        for h in heads:
            vn_ref[h, rows, :] = _dot(t16_ref[h, rows, :], x_ref[h, :, 0:GDN_DV]).astype(BF16)

    for h in heads:
        o = qs[h] + _dot(at_ref[h], vn_ref[h])
        kd16 = (k_ref[0, :, hcols(h)].astype(F32) * ek_all[:, h:h + 1]).astype(BF16)
        state_ref[h] = state_ref[h] * g_last[:, h:h + 1] + _dot_tn(kd16, vn_ref[h])
        og_ref[:, hcols(h)] = (_rms(o, on_ref[...]) * gz_ref[0, :, hcols(h)].astype(F32)).astype(BF16)

    y_out[0] = _rms(h_ref[0] + _dot(og_ref[...], wo_ref[...]), fn_ref[...])


def _gdn(q, k, v, gb, gz, h1, o_norm, w_out, final_norm):
    b, s, d = h1.shape
    c = _tile(s, GDN_CHUNK)
    masks = _inverse_masks(c)
    tok = lambda width: pl.BlockSpec((1, c, width), lambda i, j: (i, j, 0))
    const = lambda shape: pl.BlockSpec(shape, lambda i, j: (0,) * len(shape), pipeline_mode=pl.Buffered(1))
    sq = lambda dt: pltpu.VMEM((GDN_HEADS, c, c), dt)
    return pl.pallas_call(
        functools.partial(_gdn_kernel, c=c),
        grid=(b, s // c),
        in_specs=[tok(GDN_QK), tok(GDN_QK), tok(GDN_VW), tok(LANES), tok(GDN_VW), tok(d),
                  const(masks.shape), const(o_norm.shape), const(w_out.shape), const(final_norm.shape)],
        out_specs=tok(d),
        out_shape=jax.ShapeDtypeStruct((b, s, d), F32),
        scratch_shapes=[sq(F32), sq(BF16), sq(F32), sq(BF16), sq(BF16), sq(BF16),
                        pltpu.VMEM((GDN_HEADS, c, GDN_DV), F32), pltpu.VMEM((GDN_HEADS, c, GDN_DV), BF16),
                        pltpu.VMEM((c, GDN_VW), BF16), pltpu.VMEM((GDN_HEADS, GDN_DK, GDN_DV), F32)],
        compiler_params=_params(("parallel", "arbitrary")),
    )(q, k, v, gb, gz, h1, masks, o_norm, w_out, final_norm)


def _pad_cols(w, width):
    return jnp.pad(w, ((0, 0), (0, width - w.shape[1])))


def _rope_block(r, swapped):
    r1, r2 = r[:, :ROPE_HALF], r[:, ROPE_HALF:]
    z = jnp.zeros((r.shape[0], MLA_NOPE), r.dtype)
    body = jnp.concatenate([z, r2, r1] if swapped else [z, r1, r2], axis=1)
    return _pad_cols(body, LANES)


def _l0_weights(w_in, w_q_b, w_kv_b):
    o1 = MLA_Q_RANK
    o2 = o1 + MLA_KV_RANK
    o3 = o2 + MLA_ROPE
    kr = w_in[:, o2:o3]
    w_in_p = jnp.concatenate([w_in[:, :o2], _rope_block(kr, False), _rope_block(kr, True), w_in[:, o3:]], axis=1)
    qh = w_q_b.reshape(MLA_Q_RANK, MLA_HEADS, MLA_NOPE + MLA_ROPE)
    plain, swapped = [], []
    for h in range(MLA_HEADS):
        nope, rope = qh[:, h, :MLA_NOPE], qh[:, h, MLA_NOPE:]
        plain.append(_rope_block(rope, False).at[:, :MLA_NOPE].set(nope))
        swapped.append(_rope_block(rope, True))
    w_q_p = jnp.concatenate(plain + swapped, axis=1)
    kvh = w_kv_b.reshape(MLA_KV_RANK, MLA_HEADS, MLA_NOPE + MLA_V)
    k_cols = [_pad_cols(kvh[:, h, :MLA_NOPE], LANES) for h in range(MLA_HEADS)]
    v_cols = [_pad_cols(kvh[:, h, MLA_NOPE:], LANES) for h in range(MLA_HEADS)]
    w_kv_p = jnp.concatenate(k_cols + v_cols, axis=1)
    return w_in_p.astype(BF16), w_q_p.astype(BF16), w_kv_p.astype(BF16)


def kernel(x, positions, norm_ab, w_in_ab, q_a_norm, w_q_b, kv_a_norm, w_kv_b, pool_w, pool_scale, w_out_ab,
           norm_c, w_in_c, conv_w, a_log, dt_bias, o_norm, w_out_c, final_norm):
    assert norm_ab.shape[0] == 1 and norm_c.shape[0] == 1, "one even and one odd layer"
    b, s, d = x.shape

    cos, sin = _rope_tables(positions)
    ones = jnp.ones((b, s, MLA_NOPE), F32)
    tail = jnp.zeros((b, s, LANES - MLA_NOPE - MLA_ROPE), F32)
    cpat = jnp.concatenate([ones, cos, cos, tail], axis=-1)
    spat = jnp.concatenate([jnp.zeros_like(ones), sin, sin, tail], axis=-1)

    w_in_p, w_q_p, w_kv_p = _l0_weights(w_in_ab[0], w_q_b[0], w_kv_b[0])
    q, k, v, gz, yb = _l0_in(
        x, cpat, spat, norm_ab, w_in_p, q_a_norm, w_q_p, kv_a_norm, w_kv_p, pool_w[0].astype(BF16), pool_scale,
        scale=(MLA_NOPE + MLA_ROPE) ** -0.5 * math.log2(math.e))
    ya = _attention(q, k, v)

    h1, gq, gk, gv, gzc, gb = _mid(
        x, ya, gz, yb, w_out_ab[0].astype(BF16), norm_c, _pad_cols(w_in_c[0], GDN_CONV_CH + GDN_VW + LANES).astype(BF16),
        conv_w[0], _pad_cols(a_log, LANES), _pad_cols(dt_bias, LANES))
    return _gdn(gq, gk, gv, gb, gzc, h1, o_norm, w_out_c[0].astype(BF16), final_norm.reshape(1, -1))
```

```python
import functools
import math

import jax
import jax.numpy as jnp
from jax import lax
from jax.experimental import pallas as pl
from jax.experimental.pallas import tpu as pltpu

F32 = jnp.float32
BF16 = jnp.bfloat16

LANES = 128
RMS_EPS = 1e-6

MLA_HEADS = 8
MLA_Q_RANK = 256
MLA_KV_RANK = 128
MLA_NOPE = 64
MLA_ROPE = 32
MLA_V = 64
ROPE_THETA = 10000.0
ROPE_HALF = MLA_ROPE // 2

POOL_WINDOWS = (2, 4, 8, 16)
POOL_GROUP = 128
POOL_WIDTH = POOL_GROUP * len(POOL_WINDOWS)
POOL_HALO = 16

GDN_HEADS = 8
GDN_DK = 128
GDN_DV = 128
CONV_WIDTH = 4
CONV_HALO = 8
ATTN_GROUP = 4
ATTN_AHEAD = 2
MID_BLOCK = 512
GDN_CHUNK = 256
INV_BASE = 8
SOLVE_BLOCK = 64
GDN_QK = GDN_HEADS * GDN_DK
GDN_VW = GDN_HEADS * GDN_DV
GDN_CONV_CH = 2 * GDN_QK + GDN_VW

VMEM_LIMIT = 48 * 1024 * 1024
NEG_BIG = -1e30


def _tile(n, pref):
    t = min(n, pref)
    assert n % t == 0, (n, t)
    return t


def _params(sem):
    return pltpu.CompilerParams(dimension_semantics=sem, vmem_limit_bytes=VMEM_LIMIT)


def _silu(z):
    h = 0.5 * z
    return h + h * jnp.tanh(h)


def _rms(x, gain):
    ms = jnp.mean(x * x, axis=-1, keepdims=True)
    return x * lax.rsqrt(ms + RMS_EPS) * gain


def _dot(a, b):
    return jnp.dot(a, b, preferred_element_type=F32)


def _dot_nt(a, b):
    return lax.dot_general(a, b, (((1,), (1,)), ((), ())), preferred_element_type=F32)


def _dot_tn(a, b):
    return lax.dot_general(a, b, (((0,), (0,)), ((), ())), preferred_element_type=F32)


def _dot_f32(a, b):
    return jnp.dot(a, b, preferred_element_type=F32, precision=lax.Precision.HIGHEST)


def _rope_table_kernel(pos_ref, freq_ref, cos_ref, sin_ref):
    ang = pos_ref[...].astype(F32) * freq_ref[...]
    cos_ref[...] = jnp.cos(ang)
    sin_ref[...] = jnp.sin(ang)


def _rope_tables(positions):
    b, s = positions.shape
    n = b * s * ROPE_HALF
    rows = n // LANES
    pos_rep = jnp.repeat(positions.reshape(-1), ROPE_HALF).reshape(rows, LANES)
    inv_freq = 1.0 / (ROPE_THETA ** (jnp.arange(ROPE_HALF, dtype=F32) / ROPE_HALF))
    freq_row = jnp.tile(inv_freq, LANES // ROPE_HALF).reshape(1, LANES)
    tr = _tile(rows, 512)
    cos, sin = pl.pallas_call(
        _rope_table_kernel,
        grid=(rows // tr,),
        in_specs=[pl.BlockSpec((tr, LANES), lambda i: (i, 0)),
                  pl.BlockSpec((1, LANES), lambda i: (0, 0))],
        out_specs=[pl.BlockSpec((tr, LANES), lambda i: (i, 0))] * 2,
        out_shape=[jax.ShapeDtypeStruct((rows, LANES), F32)] * 2,
        compiler_params=_params(("parallel",)),
    )(pos_rep, freq_row)
    return cos.reshape(b, s, ROPE_HALF), sin.reshape(b, s, ROPE_HALF)


def _l0_in_kernel(x_ref, cp_ref, sp_ref, nrm_ref, win_ref, qn_ref, wq_ref, kvn_ref, wkv_ref, pw_ref, ps_ref,
                  q_out, k_out, v_out, gz_out, yb_out, ext_ref, hn_ref, *, tm, scale):
    t = pl.program_id(1)

    @pl.when(t == 0)
    def _():
        ext_ref[0:POOL_HALO, :] = jnp.zeros((POOL_HALO, POOL_WIDTH), F32)

    hn_ref[...] = _rms(x_ref[0], nrm_ref[...]).astype(BF16)
    o1 = MLA_Q_RANK
    o2 = o1 + MLA_KV_RANK
    o3 = o2 + LANES
    o4 = o3 + LANES
    o5 = o4 + POOL_WIDTH
    o6 = o5 + MLA_HEADS * MLA_V
    hw = MLA_HEADS * LANES
    proj = lambda lo, hi: _dot(hn_ref[...], win_ref[:, lo:hi])

    lat = proj(0, o4)
    xp = proj(o4, o5)
    zb = proj(o6, o6 + POOL_WIDTH)
    qq = _dot(_rms(lat[:, :o1], qn_ref[...]).astype(BF16), wq_ref[...])

    ext_ref[POOL_HALO:POOL_HALO + tm, :] = xp
    row = lax.broadcasted_iota(jnp.int32, (tm, 1), 0) + t * tm
    for g, w in enumerate(POOL_WINDOWS):
        lo = g * POOL_GROUP
        acc = ext_ref[:, lo:lo + POOL_GROUP]
        span = 1
        while span < w:
            acc = acc + pltpu.roll(acc, span, axis=0)
            span *= 2
        cnt = jnp.minimum(row + 1, w).astype(F32)
        d = acc[POOL_HALO:, :] / cnt - xp[:, lo:lo + POOL_GROUP]
        y = _dot(d.astype(BF16), pw_ref[g]) * ps_ref[:, lo:lo + POOL_GROUP]
        yb_out[0, :, lo:lo + POOL_GROUP] = (y * _silu(zb[:, lo:lo + POOL_GROUP])).astype(BF16)
    ext_ref[0:POOL_HALO, :] = ext_ref[tm:tm + POOL_HALO, :]

    za = proj(o5, o6)

    lane = lax.broadcasted_iota(jnp.int32, (tm, LANES), 1)
    cpat = cp_ref[0]
    spat = jnp.where(lane < MLA_NOPE + ROPE_HALF, -sp_ref[0], sp_ref[0])
    cq, sq = cpat * scale, spat * scale
    for h in range(MLA_HEADS):
        lo = h * LANES
        qh = qq[:, lo:lo + LANES] * cq + qq[:, hw + lo:hw + lo + LANES] * sq
        q_out[0, :, lo:lo + LANES] = qh.astype(BF16)

    kk = _dot(_rms(lat[:, o1:o2], kvn_ref[...]).astype(BF16), wkv_ref[...])
    gz_out[0] = _silu(za).astype(BF16)
    k_rope = lat[:, o2:o3] * cpat + lat[:, o3:o4] * spat
    for h in range(MLA_HEADS):
        lo = h * LANES
        k_out[0, :, lo:lo + LANES] = (kk[:, lo:lo + LANES] + k_rope).astype(BF16)
    ones_lane = jnp.where(lane == MLA_V, 1.0, 0.0)
    vfull = kk[:, hw:] + jnp.concatenate([ones_lane] * MLA_HEADS, axis=1)
    v_out[0] = vfull.T.astype(BF16)


def _l0_in(x, cpat, spat, norm, w_in, q_norm, w_q, kv_norm, w_kv, pool_w, pool_scale, *, scale):
    b, s, d = x.shape
    tm = _tile(s, 512)
    hw = MLA_HEADS * LANES
    vw = MLA_HEADS * MLA_V
    const = lambda shape: pl.BlockSpec(shape, lambda i, j: (0,) * len(shape))
    tok = lambda width: pl.BlockSpec((1, tm, width), lambda i, j: (i, j, 0))
    return pl.pallas_call(
        functools.partial(_l0_in_kernel, tm=tm, scale=scale),
        grid=(b, s // tm),
        in_specs=[tok(d), tok(LANES), tok(LANES), const(norm.shape), const(w_in.shape), const(q_norm.shape),
                  const(w_q.shape), const(kv_norm.shape), const(w_kv.shape), const(pool_w.shape),
                  const(pool_scale.shape)],
        out_specs=[tok(hw), tok(hw), pl.BlockSpec((1, hw, tm), lambda i, j: (i, 0, j)), tok(vw), tok(POOL_WIDTH)],
        out_shape=[jax.ShapeDtypeStruct((b, s, hw), BF16), jax.ShapeDtypeStruct((b, s, hw), BF16),
                   jax.ShapeDtypeStruct((b, hw, s), BF16), jax.ShapeDtypeStruct((b, s, vw), BF16),
                   jax.ShapeDtypeStruct((b, s, POOL_WIDTH), BF16)],
        scratch_shapes=[pltpu.VMEM((tm + POOL_HALO, POOL_WIDTH), F32), pltpu.VMEM((tm, d), BF16)],
        compiler_params=_params(("parallel", "arbitrary")),
    )(x, cpat, spat, norm, w_in, q_norm, w_q, kv_norm, w_kv, pool_w, pool_scale)


def _attn_kernel(q_ref, k_ref, v_ref, o_ref, m_ref, acc_ref, *, tq):
    qi = pl.program_id(2)
    hcols = lambda hh: slice(hh * LANES, (hh + 1) * LANES)
    m_ref[...] = jnp.full(m_ref.shape, NEG_BIG, F32)
    acc_ref[...] = jnp.zeros(acc_ref.shape, F32)

    def blocks(items):
        rows = [pl.ds(pl.multiple_of(j * tq, tq), tq) for j, _ in items]
        work = [(bi, hh) for bi in range(len(items)) for hh in range(ATTN_GROUP)]
        score = lambda bi, hh: _dot_nt(k_ref[0, rows[bi], hcols(hh)], q_ref[0, :, hcols(hh)])
        s = [score(*w) for w in work[:ATTN_AHEAD]]
        for n, (bi, hh) in enumerate(work):
            if n + ATTN_AHEAD < len(work):
                s.append(score(*work[n + ATTN_AHEAD]))
            sh = s[n]
            if items[bi][1]:
                key = lax.broadcasted_iota(jnp.int32, (tq, tq), 0)
                qry = lax.broadcasted_iota(jnp.int32, (tq, tq), 1)
                sh = jnp.where(key <= qry, sh, NEG_BIG)
            m_old = m_ref[hh]
            m_new = jnp.maximum(m_old, jnp.max(sh, axis=0, keepdims=True))
            p = jnp.exp2(sh - m_new).astype(BF16)
            acc_ref[hh] = jnp.exp2(m_old - m_new) * acc_ref[hh] + _dot(v_ref[0, hcols(hh), rows[bi]], p)
            m_ref[hh] = m_new

    def body(jj, _):
        blocks([(2 * jj, False), (2 * jj + 1, False)])
        return 0

    lax.fori_loop(0, qi // 2, body, 0)

    @pl.when(qi % 2 == 1)
    def _():
        blocks([(qi - 1, False), (qi, True)])

    @pl.when(qi % 2 == 0)
    def _():
        blocks([(qi, True)])

    lane = lax.broadcasted_iota(jnp.int32, (tq, LANES), 1)
    for hh in range(0, ATTN_GROUP, 2):
        lo, hi = [(acc_ref[h] / acc_ref[h][MLA_V:MLA_V + 1, :]).T for h in (hh, hh + 1)]
        o_ref[0, :, hcols(hh // 2)] = jnp.where(lane < MLA_V, lo, pltpu.roll(hi, MLA_V, axis=1)).astype(BF16)


def _attention(q, k, v):
    b, s, hw = q.shape
    tq = _tile(s, 512)
    gw = ATTN_GROUP * LANES
    return pl.pallas_call(
        functools.partial(_attn_kernel, tq=tq),
        grid=(b, MLA_HEADS // ATTN_GROUP, s // tq),
        in_specs=[pl.BlockSpec((1, tq, gw), lambda i, p, j: (i, j, p)),
                  pl.BlockSpec((1, s, gw), lambda i, p, j: (i, 0, p), pipeline_mode=pl.Buffered(1)),
                  pl.BlockSpec((1, gw, s), lambda i, p, j: (i, p, 0), pipeline_mode=pl.Buffered(1))],
        out_specs=pl.BlockSpec((1, tq, ATTN_GROUP * MLA_V), lambda i, p, j: (i, j, p)),
        out_shape=jax.ShapeDtypeStruct((b, s, MLA_HEADS * MLA_V), BF16),
        scratch_shapes=[pltpu.VMEM((ATTN_GROUP, 1, tq), F32), pltpu.VMEM((ATTN_GROUP, LANES, tq), F32)],
        compiler_params=_params(("parallel", "parallel", "arbitrary")),
    )(q, k, v)


def _mid_kernel(x_ref, ya_ref, gz_ref, yb_ref, wo_ref, nrm_ref, win_ref, cw_ref, alog_ref, dtb_ref,
                h_out, q_out, k_out, v_out, gzc_out, gb_out, ext_ref, hn_ref, *, tm):
    t = pl.program_id(1)
    va = MLA_HEADS * MLA_V
    ya = (ya_ref[0].astype(F32) * gz_ref[0].astype(F32)).astype(BF16)
    h1 = x_ref[0] + _dot(ya, wo_ref[0:va, :]) + _dot(yb_ref[0], wo_ref[va:, :])
    h_out[0] = h1
    hn_ref[...] = _rms(h1, nrm_ref[...]).astype(BF16)

    @pl.when(t == 0)
    def _():
        ext_ref[0:CONV_HALO, :] = jnp.zeros((CONV_HALO, GDN_CONV_CH), F32)

    def l2(blk, scale=1.0):
        return blk * (lax.rsqrt(jnp.sum(blk * blk, axis=-1, keepdims=True) + RMS_EPS) * scale)

    def conv_block(lo, xc):
        cols = slice(lo, lo + MID_BLOCK)
        ext_ref[CONV_HALO:CONV_HALO + tm, cols] = xc
        xe = ext_ref[:, cols]
        conv = xc * cw_ref[CONV_WIDTH - 1:CONV_WIDTH, cols]
        for j in range(CONV_WIDTH - 1):
            back = CONV_WIDTH - 1 - j
            conv = conv + pltpu.roll(xe, back, axis=0)[CONV_HALO:, :] * cw_ref[j:j + 1, cols]
        ext_ref[0:CONV_HALO, cols] = ext_ref[tm:tm + CONV_HALO, cols]
        act = _silu(conv)
        for i in range(MID_BLOCK // LANES):
            c0 = lo + i * LANES
            blk = act[:, i * LANES:(i + 1) * LANES]
            if c0 < GDN_QK:
                q_out[0, :, c0:c0 + LANES] = l2(blk, GDN_DK ** -0.5).astype(BF16)
            elif c0 < 2 * GDN_QK:
                k_out[0, :, c0 - GDN_QK:c0 - GDN_QK + LANES] = l2(blk).astype(BF16)
            else:
                v_out[0, :, c0 - 2 * GDN_QK:c0 - 2 * GDN_QK + LANES] = blk.astype(BF16)

    def gate_block(lo, z):
        gzc_out[0, :, lo - GDN_CONV_CH:lo - GDN_CONV_CH + MID_BLOCK] = _silu(z).astype(BF16)

    def decay_block(lo, ab):
        sp_in = ab + dtb_ref[...]
        softplus = jnp.maximum(sp_in, 0.0) + jnp.log1p(jnp.exp(-jnp.abs(sp_in)))
        g = -jnp.exp(alog_ref[...]) * softplus
        beta = 1.0 / (1.0 + jnp.exp(-ab))
        lane = lax.broadcasted_iota(jnp.int32, (tm, LANES), 1)
        gb_out[0] = jnp.where(lane < GDN_HEADS, g, beta)

    o1 = GDN_CONV_CH + GDN_VW
    stages = [(lo, MID_BLOCK, conv_block) for lo in range(0, GDN_CONV_CH, MID_BLOCK)]
    stages += [(lo, MID_BLOCK, gate_block) for lo in range(GDN_CONV_CH, o1, MID_BLOCK)]
    stages += [(o1, LANES, decay_block)]
    pending = None
    for lo, width, consume in stages:
        proj = _dot(hn_ref[...], win_ref[:, lo:lo + width])
        if pending is not None:
            pending[0](pending[1], pending[2])
        pending = (consume, lo, proj)
    pending[0](pending[1], pending[2])


def _mid(x, ya, gz, yb, w_out, norm, w_in, conv_w, a_log, dt_bias):
    b, s, d = x.shape
    tm = _tile(s, 512)
    const = lambda shape: pl.BlockSpec(shape, lambda i, j: (0,) * len(shape), pipeline_mode=pl.Buffered(1))
    tok = lambda width: pl.BlockSpec((1, tm, width), lambda i, j: (i, j, 0))
    va = MLA_HEADS * MLA_V
    return pl.pallas_call(
        functools.partial(_mid_kernel, tm=tm),
        grid=(b, s // tm),
        in_specs=[tok(d), tok(va), tok(va), tok(POOL_WIDTH), const(w_out.shape), const(norm.shape),
                  const(w_in.shape), const(conv_w.shape), const(a_log.shape), const(dt_bias.shape)],
        out_specs=[tok(d), tok(GDN_QK), tok(GDN_QK), tok(GDN_VW), tok(GDN_VW), tok(LANES)],
        out_shape=[jax.ShapeDtypeStruct((b, s, d), F32), jax.ShapeDtypeStruct((b, s, GDN_QK), BF16),
                   jax.ShapeDtypeStruct((b, s, GDN_QK), BF16), jax.ShapeDtypeStruct((b, s, GDN_VW), BF16),
                   jax.ShapeDtypeStruct((b, s, GDN_VW), BF16), jax.ShapeDtypeStruct((b, s, LANES), F32)],
        scratch_shapes=[pltpu.VMEM((tm + CONV_HALO, GDN_CONV_CH), F32), pltpu.VMEM((tm, d), BF16)],
        compiler_params=_params(("parallel", "arbitrary")),
    )(x, ya, gz, yb, w_out, norm, w_in, conv_w, a_log, dt_bias)


def _chunk_cumsum(g):
    n = g.shape[0]
    row = lax.broadcasted_iota(jnp.int32, g.shape, 0)
    sh = 1
    while sh < n:
        g = g + jnp.where(row >= sh, pltpu.roll(g, sh, axis=0), 0.0)
        sh *= 2
    return g


def _inverse_masks(c):
    idx = jnp.arange(c)
    blk = lambda size: (idx[:, None] // size) == (idx[None, :] // size)
    masks = [blk(INV_BASE)]
    size = INV_BASE
    while size < SOLVE_BLOCK:
        masks.append(blk(2 * size) & ~blk(size) & (idx[:, None] > idx[None, :]))
        size *= 2
    return jnp.stack(masks).astype(F32)


def _gdn_kernel(q_ref, k_ref, v_ref, gb_ref, gz_ref, h_ref, msk_ref, on_ref, wo_ref, fn_ref, y_out,
                m_ref, m16_ref, t_ref, t16_ref, x_ref, at_ref, rhs_ref, vn_ref, og_ref, state_ref, *, c):
    heads = range(GDN_HEADS)
    hcols = lambda h: slice(h * LANES, (h + 1) * LANES)
    r = lax.broadcasted_iota(jnp.int32, (c, c), 0)
    col = lax.broadcasted_iota(jnp.int32, (c, c), 1)

    @pl.when(pl.program_id(1) == 0)
    def _():
        state_ref[...] = jnp.zeros(state_ref.shape, F32)

    gb = gb_ref[0]
    gc_all = _chunk_cumsum(gb)
    gc_t = gc_all.T
    gc_last = gc_all[c - 1:c, :]
    e_all = jnp.exp(gc_all)
    ek_all = jnp.exp(gc_last - gc_all)
    g_last = jnp.exp(gc_last)
    beta = lambda h: gb[:, GDN_HEADS + h:GDN_HEADS + h + 1]

    for h in heads:
        gamma = jnp.exp(jnp.where(r >= col, gc_all[:, h:h + 1] - gc_t[h:h + 1, :], -jnp.inf))
        k16 = k_ref[0, :, hcols(h)]
        kb16 = (k16.astype(F32) * beta(h)).astype(BF16)
        m = jnp.where(r > col, _dot_nt(kb16, k16) * gamma, 0.0)
        m_ref[h] = m
        m16_ref[h] = m.astype(BF16)
        at_ref[h] = (_dot_nt(q_ref[0, :, hcols(h)], k16) * gamma).astype(BF16)

    for h in heads:
        p = m_ref[h] * msk_ref[0]
        t_ref[h] = jnp.where(r == col, 1.0, 0.0) - p
        x_ref[h] = p.astype(BF16)
    span = 2
    while span < INV_BASE:
        for h in heads:
            p16 = x_ref[h]
            p = _dot(p16, p16)
            t = t_ref[h]
            t_ref[h] = t + _dot(t.astype(BF16), p.astype(BF16))
            x_ref[h] = p.astype(BF16)
        span *= 2
    for h in heads:
        t16_ref[h] = t_ref[h].astype(BF16)

    level = 1
    size = INV_BASE
    while size < SOLVE_BLOCK:
        for h in heads:
            off = (m_ref[h] * msk_ref[level]).astype(BF16)
            x_ref[h] = _dot(off, t16_ref[h]).astype(BF16)
        for h in heads:
            t = t_ref[h] - _dot(t16_ref[h], x_ref[h])
            t_ref[h] = t
            t16_ref[h] = t.astype(BF16)
        size *= 2
        level += 1

    st16 = [state_ref[h].astype(BF16) for h in heads]
    qs = []
    for h in heads:
        kf = k_ref[0, :, hcols(h)].astype(F32)
        kbe16 = (kf * (beta(h) * e_all[:, h:h + 1])).astype(BF16)
        rhs = v_ref[0, :, hcols(h)].astype(F32) * beta(h) - _dot(kbe16, st16[h])
        rhs_ref[h] = rhs
        x_ref[h, :, 0:GDN_DV] = rhs.astype(BF16)
        vn_ref[h] = jnp.zeros((c, GDN_DV), BF16)
        qd16 = (q_ref[0, :, hcols(h)].astype(F32) * e_all[:, h:h + 1]).astype(BF16)
        qs.append(_dot(qd16, st16[h]))

    for i in range(c // SOLVE_BLOCK):
        rows = slice(i * SOLVE_BLOCK, (i + 1) * SOLVE_BLOCK)
        if i > 0:
            for h in heads:
                rhs = rhs_ref[h, rows, :] - _dot(m16_ref[h, rows, :], vn_ref[h])
                x_ref[h, rows, 0:GDN_DV] = rhs.astype(BF16)
        for h in heads:
            vn_ref[h, rows, :] = _dot(t16_ref[h, rows, :], x_ref[h, :, 0:GDN_DV]).astype(BF16)

    for h in heads:
        o = qs[h] + _dot(at_ref[h], vn_ref[h])
        kd16 = (k_ref[0, :, hcols(h)].astype(F32) * ek_all[:, h:h + 1]).astype(BF16)
        state_ref[h] = state_ref[h] * g_last[:, h:h + 1] + _dot_tn(kd16, vn_ref[h])
        og_ref[:, hcols(h)] = (_rms(o, on_ref[...]) * gz_ref[0, :, hcols(h)].astype(F32)).astype(BF16)

    y_out[0] = _rms(h_ref[0] + _dot(og_ref[...], wo_ref[...]), fn_ref[...])


def _gdn(q, k, v, gb, gz, h1, o_norm, w_out, final_norm):
    b, s, d = h1.shape
    c = _tile(s, GDN_CHUNK)
    masks = _inverse_masks(c)
    tok = lambda width: pl.BlockSpec((1, c, width), lambda i, j: (i, j, 0))
    const = lambda shape: pl.BlockSpec(shape, lambda i, j: (0,) * len(shape), pipeline_mode=pl.Buffered(1))
    sq = lambda dt: pltpu.VMEM((GDN_HEADS, c, c), dt)
    return pl.pallas_call(
        functools.partial(_gdn_kernel, c=c),
        grid=(b, s // c),
        in_specs=[tok(GDN_QK), tok(GDN_QK), tok(GDN_VW), tok(LANES), tok(GDN_VW), tok(d),
                  const(masks.shape), const(o_norm.shape), const(w_out.shape), const(final_norm.shape)],
        out_specs=tok(d),
        out_shape=jax.ShapeDtypeStruct((b, s, d), F32),
        scratch_shapes=[sq(F32), sq(BF16), sq(F32), sq(BF16), sq(BF16), sq(BF16),
                        pltpu.VMEM((GDN_HEADS, c, GDN_DV), F32), pltpu.VMEM((GDN_HEADS, c, GDN_DV), BF16),
                        pltpu.VMEM((c, GDN_VW), BF16), pltpu.VMEM((GDN_HEADS, GDN_DK, GDN_DV), F32)],
        compiler_params=_params(("parallel", "arbitrary")),
    )(q, k, v, gb, gz, h1, masks, o_norm, w_out, final_norm)


def _pad_cols(w, width):
    return jnp.pad(w, ((0, 0), (0, width - w.shape[1])))


def _rope_block(r, swapped):
    r1, r2 = r[:, :ROPE_HALF], r[:, ROPE_HALF:]
    z = jnp.zeros((r.shape[0], MLA_NOPE), r.dtype)
    body = jnp.concatenate([z, r2, r1] if swapped else [z, r1, r2], axis=1)
    return _pad_cols(body, LANES)


def _l0_weights(w_in, w_q_b, w_kv_b):
    o1 = MLA_Q_RANK
    o2 = o1 + MLA_KV_RANK
    o3 = o2 + MLA_ROPE
    kr = w_in[:, o2:o3]
    w_in_p = jnp.concatenate([w_in[:, :o2], _rope_block(kr, False), _rope_block(kr, True), w_in[:, o3:]], axis=1)
    qh = w_q_b.reshape(MLA_Q_RANK, MLA_HEADS, MLA_NOPE + MLA_ROPE)
    plain, swapped = [], []
    for h in range(MLA_HEADS):
        nope, rope = qh[:, h, :MLA_NOPE], qh[:, h, MLA_NOPE:]
        plain.append(_rope_block(rope, False).at[:, :MLA_NOPE].set(nope))
        swapped.append(_rope_block(rope, True))
    w_q_p = jnp.concatenate(plain + swapped, axis=1)
    kvh = w_kv_b.reshape(MLA_KV_RANK, MLA_HEADS, MLA_NOPE + MLA_V)
    k_cols = [_pad_cols(kvh[:, h, :MLA_NOPE], LANES) for h in range(MLA_HEADS)]
    v_cols = [_pad_cols(kvh[:, h, MLA_NOPE:], LANES) for h in range(MLA_HEADS)]
    w_kv_p = jnp.concatenate(k_cols + v_cols, axis=1)
    return w_in_p.astype(BF16), w_q_p.astype(BF16), w_kv_p.astype(BF16)


def kernel(x, positions, norm_ab, w_in_ab, q_a_norm, w_q_b, kv_a_norm, w_kv_b, pool_w, pool_scale, w_out_ab,
           norm_c, w_in_c, conv_w, a_log, dt_bias, o_norm, w_out_c, final_norm):
    assert norm_ab.shape[0] == 1 and norm_c.shape[0] == 1, "one even and one odd layer"
    b, s, d = x.shape

    cos, sin = _rope_tables(positions)
    ones = jnp.ones((b, s, MLA_NOPE), F32)
    tail = jnp.zeros((b, s, LANES - MLA_NOPE - MLA_ROPE), F32)
    cpat = jnp.concatenate([ones, cos, cos, tail], axis=-1)
    spat = jnp.concatenate([jnp.zeros_like(ones), sin, sin, tail], axis=-1)

    w_in_p, w_q_p, w_kv_p = _l0_weights(w_in_ab[0], w_q_b[0], w_kv_b[0])
    q, k, v, gz, yb = _l0_in(
        x, cpat, spat, norm_ab, w_in_p, q_a_norm, w_q_p, kv_a_norm, w_kv_p, pool_w[0].astype(BF16), pool_scale,
        scale=(MLA_NOPE + MLA_ROPE) ** -0.5 * math.log2(math.e))
    ya = _attention(q, k, v)

    h1, gq, gk, gv, gzc, gb = _mid(
        x, ya, gz, yb, w_out_ab[0].astype(BF16), norm_c, _pad_cols(w_in_c[0], GDN_CONV_CH + GDN_VW + LANES).astype(BF16),
        conv_w[0], _pad_cols(a_log, LANES), _pad_cols(dt_bias, LANES))
    return _gdn(gq, gk, gv, gb, gzc, h1, o_norm, w_out_c[0].astype(BF16), final_norm.reshape(1, -1))
```

```python
import functools
import math

import jax
import jax.numpy as jnp
from jax import lax
from jax.experimental import pallas as pl
from jax.experimental.pallas import tpu as pltpu

F32 = jnp.float32
BF16 = jnp.bfloat16

LANES = 128
RMS_EPS = 1e-6

MLA_HEADS = 8
MLA_Q_RANK = 256
MLA_KV_RANK = 128
MLA_NOPE = 64
MLA_ROPE = 32
MLA_V = 64
ROPE_THETA = 10000.0
ROPE_HALF = MLA_ROPE // 2

POOL_WINDOWS = (2, 4, 8, 16)
POOL_GROUP = 128
POOL_WIDTH = POOL_GROUP * len(POOL_WINDOWS)
POOL_HALO = 16

GDN_HEADS = 8
GDN_DK = 128
GDN_DV = 128
CONV_WIDTH = 4
CONV_HALO = 8
ATTN_GROUP = 4
ATTN_AHEAD = 3
MID_BLOCK = 512
GDN_CHUNK = 256
INV_BASE = 8
SOLVE_BLOCK = 64
GDN_QK = GDN_HEADS * GDN_DK
GDN_VW = GDN_HEADS * GDN_DV
GDN_CONV_CH = 2 * GDN_QK + GDN_VW

VMEM_LIMIT = 48 * 1024 * 1024
NEG_BIG = -1e30


def _tile(n, pref):
    t = min(n, pref)
    assert n % t == 0, (n, t)
    return t


def _params(sem):
    return pltpu.CompilerParams(dimension_semantics=sem, vmem_limit_bytes=VMEM_LIMIT)


def _silu(z):
    h = 0.5 * z
    return h + h * jnp.tanh(h)


def _rms(x, gain):
    ms = jnp.mean(x * x, axis=-1, keepdims=True)
    return x * lax.rsqrt(ms + RMS_EPS) * gain


def _dot(a, b):
    return jnp.dot(a, b, preferred_element_type=F32)


def _dot_nt(a, b):
    return lax.dot_general(a, b, (((1,), (1,)), ((), ())), preferred_element_type=F32)


def _dot_tn(a, b):
    return lax.dot_general(a, b, (((0,), (0,)), ((), ())), preferred_element_type=F32)


def _dot_f32(a, b):
    return jnp.dot(a, b, preferred_element_type=F32, precision=lax.Precision.HIGHEST)


def _rope_table_kernel(pos_ref, freq_ref, cos_ref, sin_ref):
    ang = pos_ref[...].astype(F32) * freq_ref[...]
    cos_ref[...] = jnp.cos(ang)
    sin_ref[...] = jnp.sin(ang)


def _rope_tables(positions):
    b, s = positions.shape
    n = b * s * ROPE_HALF
    rows = n // LANES
    pos_rep = jnp.repeat(positions.reshape(-1), ROPE_HALF).reshape(rows, LANES)
    inv_freq = 1.0 / (ROPE_THETA ** (jnp.arange(ROPE_HALF, dtype=F32) / ROPE_HALF))
    freq_row = jnp.tile(inv_freq, LANES // ROPE_HALF).reshape(1, LANES)
    tr = _tile(rows, 512)
    cos, sin = pl.pallas_call(
        _rope_table_kernel,
        grid=(rows // tr,),
        in_specs=[pl.BlockSpec((tr, LANES), lambda i: (i, 0)),
                  pl.BlockSpec((1, LANES), lambda i: (0, 0))],
        out_specs=[pl.BlockSpec((tr, LANES), lambda i: (i, 0))] * 2,
        out_shape=[jax.ShapeDtypeStruct((rows, LANES), F32)] * 2,
        compiler_params=_params(("parallel",)),
    )(pos_rep, freq_row)
    return cos.reshape(b, s, ROPE_HALF), sin.reshape(b, s, ROPE_HALF)


def _l0_in_kernel(x_ref, cp_ref, sp_ref, nrm_ref, win_ref, qn_ref, wq_ref, kvn_ref, wkv_ref, pw_ref, ps_ref,
                  q_out, k_out, v_out, gz_out, yb_out, ext_ref, hn_ref, *, tm, scale):
    t = pl.program_id(1)

    @pl.when(t == 0)
    def _():
        ext_ref[0:POOL_HALO, :] = jnp.zeros((POOL_HALO, POOL_WIDTH), F32)

    hn_ref[...] = _rms(x_ref[0], nrm_ref[...]).astype(BF16)
    o1 = MLA_Q_RANK
    o2 = o1 + MLA_KV_RANK
    o3 = o2 + LANES
    o4 = o3 + LANES
    o5 = o4 + POOL_WIDTH
    o6 = o5 + MLA_HEADS * MLA_V
    hw = MLA_HEADS * LANES
    proj = lambda lo, hi: _dot(hn_ref[...], win_ref[:, lo:hi])

    lat = proj(0, o4)
    xp = proj(o4, o5)
    zb = proj(o6, o6 + POOL_WIDTH)
    qq = _dot(_rms(lat[:, :o1], qn_ref[...]).astype(BF16), wq_ref[...])

    ext_ref[POOL_HALO:POOL_HALO + tm, :] = xp
    row = lax.broadcasted_iota(jnp.int32, (tm, 1), 0) + t * tm
    for g, w in enumerate(POOL_WINDOWS):
        lo = g * POOL_GROUP
        acc = ext_ref[:, lo:lo + POOL_GROUP]
        span = 1
        while span < w:
            acc = acc + pltpu.roll(acc, span, axis=0)
            span *= 2
        cnt = jnp.minimum(row + 1, w).astype(F32)
        d = acc[POOL_HALO:, :] / cnt - xp[:, lo:lo + POOL_GROUP]
        y = _dot(d.astype(BF16), pw_ref[g]) * ps_ref[:, lo:lo + POOL_GROUP]
        yb_out[0, :, lo:lo + POOL_GROUP] = (y * _silu(zb[:, lo:lo + POOL_GROUP])).astype(BF16)
    ext_ref[0:POOL_HALO, :] = ext_ref[tm:tm + POOL_HALO, :]

    za = proj(o5, o6)

    lane = lax.broadcasted_iota(jnp.int32, (tm, LANES), 1)
    cpat = cp_ref[0]
    spat = jnp.where(lane < MLA_NOPE + ROPE_HALF, -sp_ref[0], sp_ref[0])
    cq, sq = cpat * scale, spat * scale
    for h in range(MLA_HEADS):
        lo = h * LANES
        qh = qq[:, lo:lo + LANES] * cq + qq[:, hw + lo:hw + lo + LANES] * sq
        q_out[0, :, lo:lo + LANES] = qh.astype(BF16)

    kk = _dot(_rms(lat[:, o1:o2], kvn_ref[...]).astype(BF16), wkv_ref[...])
    gz_out[0] = _silu(za).astype(BF16)
    k_rope = lat[:, o2:o3] * cpat + lat[:, o3:o4] * spat
    for h in range(MLA_HEADS):
        lo = h * LANES
        k_out[0, :, lo:lo + LANES] = (kk[:, lo:lo + LANES] + k_rope).astype(BF16)
    ones_lane = jnp.where(lane == MLA_V, 1.0, 0.0)
    for h in range(MLA_HEADS):
        lo = h * LANES
        v_out[0, :, lo:lo + LANES] = (kk[:, hw + lo:hw + lo + LANES] + ones_lane).astype(BF16)


def _l0_in(x, cpat, spat, norm, w_in, q_norm, w_q, kv_norm, w_kv, pool_w, pool_scale, *, scale):
    b, s, d = x.shape
    tm = _tile(s, 512)
    hw = MLA_HEADS * LANES
    vw = MLA_HEADS * MLA_V
    const = lambda shape: pl.BlockSpec(shape, lambda i, j: (0,) * len(shape))
    tok = lambda width: pl.BlockSpec((1, tm, width), lambda i, j: (i, j, 0))
    return pl.pallas_call(
        functools.partial(_l0_in_kernel, tm=tm, scale=scale),
        grid=(b, s // tm),
        in_specs=[tok(d), tok(LANES), tok(LANES), const(norm.shape), const(w_in.shape), const(q_norm.shape),
                  const(w_q.shape), const(kv_norm.shape), const(w_kv.shape), const(pool_w.shape),
                  const(pool_scale.shape)],
        out_specs=[tok(hw), tok(hw), tok(hw), tok(vw), tok(POOL_WIDTH)],
        out_shape=[jax.ShapeDtypeStruct((b, s, hw), BF16), jax.ShapeDtypeStruct((b, s, hw), BF16),
                   jax.ShapeDtypeStruct((b, s, hw), BF16), jax.ShapeDtypeStruct((b, s, vw), BF16),
                   jax.ShapeDtypeStruct((b, s, POOL_WIDTH), BF16)],
        scratch_shapes=[pltpu.VMEM((tm + POOL_HALO, POOL_WIDTH), F32), pltpu.VMEM((tm, d), BF16)],
        compiler_params=_params(("parallel", "arbitrary")),
    )(x, cpat, spat, norm, w_in, q_norm, w_q, kv_norm, w_kv, pool_w, pool_scale)


def _attn_kernel(q_ref, k_ref, v_ref, o_ref, m_ref, acc_ref, *, tq):
    qi = pl.program_id(2)
    hcols = lambda hh: slice(hh * LANES, (hh + 1) * LANES)
    m_ref[...] = jnp.full(m_ref.shape, NEG_BIG, F32)
    acc_ref[...] = jnp.zeros(acc_ref.shape, F32)

    def blocks(items):
        rows = [pl.ds(pl.multiple_of(j * tq, tq), tq) for j, _ in items]
        work = [(bi, hh) for bi in range(len(items)) for hh in range(ATTN_GROUP)]
        score = lambda bi, hh: _dot_nt(q_ref[0, :, hcols(hh)], k_ref[0, rows[bi], hcols(hh)])
        s = [score(*w) for w in work[:ATTN_AHEAD]]
        for n, (bi, hh) in enumerate(work):
            if n + ATTN_AHEAD < len(work):
                s.append(score(*work[n + ATTN_AHEAD]))
            sh = s[n]
            if items[bi][1]:
                r = lax.broadcasted_iota(jnp.int32, (tq, tq), 0)
                c = lax.broadcasted_iota(jnp.int32, (tq, tq), 1)
                sh = jnp.where(c <= r, sh, NEG_BIG)
            m_old = m_ref[hh]
            m_new = jnp.maximum(m_old, jnp.max(sh, axis=-1, keepdims=True))
            p = jnp.concatenate([jnp.exp2(sh[:, c0:c0 + LANES] - m_new) for c0 in range(0, tq, LANES)], axis=1)
            acc_ref[hh] = (jnp.exp2(m_old - m_new) * acc_ref[hh]
                           + _dot(p.astype(BF16), v_ref[0, rows[bi], hcols(hh)]))
            m_ref[hh] = m_new

    def body(jj, _):
        blocks([(2 * jj, False), (2 * jj + 1, False)])
        return 0

    lax.fori_loop(0, qi // 2, body, 0)

    @pl.when(qi % 2 == 1)
    def _():
        blocks([(qi - 1, False), (qi, True)])

    @pl.when(qi % 2 == 0)
    def _():
        blocks([(qi, True)])

    lane = lax.broadcasted_iota(jnp.int32, (tq, LANES), 1)
    for hh in range(0, ATTN_GROUP, 2):
        lo, hi = [acc_ref[h] / acc_ref[h][:, MLA_V:MLA_V + 1] for h in (hh, hh + 1)]
        o_ref[0, :, hcols(hh // 2)] = jnp.where(lane < MLA_V, lo, pltpu.roll(hi, MLA_V, axis=1)).astype(BF16)


def _attention(q, k, v):
    b, s, hw = q.shape
    tq = _tile(s, 512)
    gw = ATTN_GROUP * LANES
    whole = pl.BlockSpec((1, s, gw), lambda i, p, j: (i, 0, p))
    return pl.pallas_call(
        functools.partial(_attn_kernel, tq=tq),
        grid=(b, MLA_HEADS // ATTN_GROUP, s // tq),
        in_specs=[pl.BlockSpec((1, tq, gw), lambda i, p, j: (i, j, p)), whole, whole],
        out_specs=pl.BlockSpec((1, tq, ATTN_GROUP * MLA_V), lambda i, p, j: (i, j, p)),
        out_shape=jax.ShapeDtypeStruct((b, s, MLA_HEADS * MLA_V), BF16),
        scratch_shapes=[pltpu.VMEM((ATTN_GROUP, tq, LANES), F32), pltpu.VMEM((ATTN_GROUP, tq, LANES), F32)],
        compiler_params=_params(("parallel", "parallel", "arbitrary")),
    )(q, k, v)


def _mid_kernel(x_ref, ya_ref, gz_ref, yb_ref, wo_ref, nrm_ref, win_ref, cw_ref, alog_ref, dtb_ref,
                h_out, q_out, k_out, v_out, gzc_out, gb_out, ext_ref, hn_ref, *, tm):
    t = pl.program_id(1)
    va = MLA_HEADS * MLA_V
    ya = (ya_ref[0].astype(F32) * gz_ref[0].astype(F32)).astype(BF16)
    h1 = x_ref[0] + _dot(ya, wo_ref[0:va, :]) + _dot(yb_ref[0], wo_ref[va:, :])
    h_out[0] = h1
    hn_ref[...] = _rms(h1, nrm_ref[...]).astype(BF16)

    @pl.when(t == 0)
    def _():
        ext_ref[0:CONV_HALO, :] = jnp.zeros((CONV_HALO, GDN_CONV_CH), F32)

    def l2(blk, scale=1.0):
        return blk * (lax.rsqrt(jnp.sum(blk * blk, axis=-1, keepdims=True) + RMS_EPS) * scale)

    def conv_block(lo, xc):
        cols = slice(lo, lo + MID_BLOCK)
        ext_ref[CONV_HALO:CONV_HALO + tm, cols] = xc
        xe = ext_ref[:, cols]
        conv = xc * cw_ref[CONV_WIDTH - 1:CONV_WIDTH, cols]
        for j in range(CONV_WIDTH - 1):
            back = CONV_WIDTH - 1 - j
            conv = conv + pltpu.roll(xe, back, axis=0)[CONV_HALO:, :] * cw_ref[j:j + 1, cols]
        ext_ref[0:CONV_HALO, cols] = ext_ref[tm:tm + CONV_HALO, cols]
        act = _silu(conv)
        for i in range(MID_BLOCK // LANES):
            c0 = lo + i * LANES
            blk = act[:, i * LANES:(i + 1) * LANES]
            if c0 < GDN_QK:
                q_out[0, :, c0:c0 + LANES] = l2(blk, GDN_DK ** -0.5).astype(BF16)
            elif c0 < 2 * GDN_QK:
                k_out[0, :, c0 - GDN_QK:c0 - GDN_QK + LANES] = l2(blk).astype(BF16)
            else:
                v_out[0, :, c0 - 2 * GDN_QK:c0 - 2 * GDN_QK + LANES] = blk.astype(BF16)

    def gate_block(lo, z):
        gzc_out[0, :, lo - GDN_CONV_CH:lo - GDN_CONV_CH + MID_BLOCK] = _silu(z).astype(BF16)

    def decay_block(lo, ab):
        sp_in = ab + dtb_ref[...]
        softplus = jnp.maximum(sp_in, 0.0) + jnp.log1p(jnp.exp(-jnp.abs(sp_in)))
        g = -jnp.exp(alog_ref[...]) * softplus
        beta = 1.0 / (1.0 + jnp.exp(-ab))
        lane = lax.broadcasted_iota(jnp.int32, (tm, LANES), 1)
        gb_out[0] = jnp.where(lane < GDN_HEADS, g, beta)

    o1 = GDN_CONV_CH + GDN_VW
    stages = [(lo, MID_BLOCK, conv_block) for lo in range(0, GDN_CONV_CH, MID_BLOCK)]
    stages += [(lo, MID_BLOCK, gate_block) for lo in range(GDN_CONV_CH, o1, MID_BLOCK)]
    stages += [(o1, LANES, decay_block)]
    pending = None
    for lo, width, consume in stages:
        proj = _dot(hn_ref[...], win_ref[:, lo:lo + width])
        if pending is not None:
            pending[0](pending[1], pending[2])
        pending = (consume, lo, proj)
    pending[0](pending[1], pending[2])


def _mid(x, ya, gz, yb, w_out, norm, w_in, conv_w, a_log, dt_bias):
    b, s, d = x.shape
    tm = _tile(s, 512)
    const = lambda shape: pl.BlockSpec(shape, lambda i, j: (0,) * len(shape), pipeline_mode=pl.Buffered(1))
    tok = lambda width: pl.BlockSpec((1, tm, width), lambda i, j: (i, j, 0))
    va = MLA_HEADS * MLA_V
    return pl.pallas_call(
        functools.partial(_mid_kernel, tm=tm),
        grid=(b, s // tm),
        in_specs=[tok(d), tok(va), tok(va), tok(POOL_WIDTH), const(w_out.shape), const(norm.shape),
                  const(w_in.shape), const(conv_w.shape), const(a_log.shape), const(dt_bias.shape)],
        out_specs=[tok(d), tok(GDN_QK), tok(GDN_QK), tok(GDN_VW), tok(GDN_VW), tok(LANES)],
        out_shape=[jax.ShapeDtypeStruct((b, s, d), F32), jax.ShapeDtypeStruct((b, s, GDN_QK), BF16),
                   jax.ShapeDtypeStruct((b, s, GDN_QK), BF16), jax.ShapeDtypeStruct((b, s, GDN_VW), BF16),
                   jax.ShapeDtypeStruct((b, s, GDN_VW), BF16), jax.ShapeDtypeStruct((b, s, LANES), F32)],
        scratch_shapes=[pltpu.VMEM((tm + CONV_HALO, GDN_CONV_CH), F32), pltpu.VMEM((tm, d), BF16)],
        compiler_params=_params(("parallel", "arbitrary")),
    )(x, ya, gz, yb, w_out, norm, w_in, conv_w, a_log, dt_bias)


def _chunk_cumsum(g):
    n = g.shape[0]
    row = lax.broadcasted_iota(jnp.int32, g.shape, 0)
    sh = 1
    while sh < n:
        g = g + jnp.where(row >= sh, pltpu.roll(g, sh, axis=0), 0.0)
        sh *= 2
    return g


def _inverse_masks(c):
    idx = jnp.arange(c)
    blk = lambda size: (idx[:, None] // size) == (idx[None, :] // size)
    masks = [blk(INV_BASE)]
    size = INV_BASE
    while size < SOLVE_BLOCK:
        masks.append(blk(2 * size) & ~blk(size) & (idx[:, None] > idx[None, :]))
        size *= 2
    return jnp.stack(masks).astype(F32)


def _gdn_kernel(q_ref, k_ref, v_ref, gb_ref, gz_ref, h_ref, msk_ref, on_ref, wo_ref, fn_ref, y_out,
                m_ref, m16_ref, t_ref, t16_ref, x_ref, at_ref, rhs_ref, vn_ref, og_ref, state_ref, *, c):
    heads = range(GDN_HEADS)
    hcols = lambda h: slice(h * LANES, (h + 1) * LANES)
    r = lax.broadcasted_iota(jnp.int32, (c, c), 0)
    col = lax.broadcasted_iota(jnp.int32, (c, c), 1)

    @pl.when(pl.program_id(1) == 0)
    def _():
        state_ref[...] = jnp.zeros(state_ref.shape, F32)

    gb = gb_ref[0]
    gc_all = _chunk_cumsum(gb)
    gc_t = gc_all.T
    gc_last = gc_all[c - 1:c, :]
    e_all = jnp.exp(gc_all)
    ek_all = jnp.exp(gc_last - gc_all)
    g_last = jnp.exp(gc_last)
    beta = lambda h: gb[:, GDN_HEADS + h:GDN_HEADS + h + 1]

    for h in heads:
        gamma = jnp.exp(jnp.where(r >= col, gc_all[:, h:h + 1] - gc_t[h:h + 1, :], -jnp.inf))
        k16 = k_ref[0, :, hcols(h)]
        kb16 = (k16.astype(F32) * beta(h)).astype(BF16)
        m = jnp.where(r > col, _dot_nt(kb16, k16) * gamma, 0.0)
        m_ref[h] = m
        m16_ref[h] = m.astype(BF16)
        at_ref[h] = (_dot_nt(q_ref[0, :, hcols(h)], k16) * gamma).astype(BF16)

    for h in heads:
        p = m_ref[h] * msk_ref[0]
        t_ref[h] = jnp.where(r == col, 1.0, 0.0) - p
        x_ref[h] = p.astype(BF16)
    span = 2
    while span < INV_BASE:
        for h in heads:
            p16 = x_ref[h]
            p = _dot(p16, p16)
            t = t_ref[h]
            t_ref[h] = t + _dot(t.astype(BF16), p.astype(BF16))
            x_ref[h] = p.astype(BF16)
        span *= 2
    for h in heads:
        t16_ref[h] = t_ref[h].astype(BF16)

    level = 1
    size = INV_BASE
    while size < SOLVE_BLOCK:
        for h in heads:
            off = (m_ref[h] * msk_ref[level]).astype(BF16)
            x_ref[h] = _dot(off, t16_ref[h]).astype(BF16)
        for h in heads:
            t = t_ref[h] - _dot(t16_ref[h], x_ref[h])
            t_ref[h] = t
            t16_ref[h] = t.astype(BF16)
        size *= 2
        level += 1

    st16 = [state_ref[h].astype(BF16) for h in heads]
    qs = []
    for h in heads:
        kf = k_ref[0, :, hcols(h)].astype(F32)
        kbe16 = (kf * (beta(h) * e_all[:, h:h + 1])).astype(BF16)
        rhs = v_ref[0, :, hcols(h)].astype(F32) * beta(h) - _dot(kbe16, st16[h])
        rhs_ref[h] = rhs
        x_ref[h, :, 0:GDN_DV] = rhs.astype(BF16)
        vn_ref[h] = jnp.zeros((c, GDN_DV), BF16)
        qd16 = (q_ref[0, :, hcols(h)].astype(F32) * e_all[:, h:h + 1]).astype(BF16)
        qs.append(_dot(qd16, st16[h]))

    for i in range(c // SOLVE_BLOCK):
        rows = slice(i * SOLVE_BLOCK, (i + 1) * SOLVE_BLOCK)
        if i > 0:
            for h in heads:
                rhs = rhs_ref[h, rows, :] - _dot(m16_ref[h, rows, :], vn_ref[h])
                x_ref[h, rows, 0:GDN_DV] = rhs.astype(BF16)
        for h in heads:
            vn_ref[h, rows, :] = _dot(t16_ref[h, rows, :], x_ref[h, :, 0:GDN_DV]).astype(BF16)

    for h in heads:
        o = qs[h] + _dot(at_ref[h], vn_ref[h])
        kd16 = (k_ref[0, :, hcols(h)].astype(F32) * ek_all[:, h:h + 1]).astype(BF16)
        state_ref[h] = state_ref[h] * g_last[:, h:h + 1] + _dot_tn(kd16, vn_ref[h])
        og_ref[:, hcols(h)] = (_rms(o, on_ref[...]) * gz_ref[0, :, hcols(h)].astype(F32)).astype(BF16)

    y_out[0] = _rms(h_ref[0] + _dot(og_ref[...], wo_ref[...]), fn_ref[...])


def _gdn(q, k, v, gb, gz, h1, o_norm, w_out, final_norm):
    b, s, d = h1.shape
    c = _tile(s, GDN_CHUNK)
    masks = _inverse_masks(c)
    tok = lambda width: pl.BlockSpec((1, c, width), lambda i, j: (i, j, 0))
    const = lambda shape: pl.BlockSpec(shape, lambda i, j: (0,) * len(shape), pipeline_mode=pl.Buffered(1))
    sq = lambda dt: pltpu.VMEM((GDN_HEADS, c, c), dt)
    return pl.pallas_call(
        functools.partial(_gdn_kernel, c=c),
        grid=(b, s // c),
        in_specs=[tok(GDN_QK), tok(GDN_QK), tok(GDN_VW), tok(LANES), tok(GDN_VW), tok(d),
                  const(masks.shape), const(o_norm.shape), const(w_out.shape), const(final_norm.shape)],
        out_specs=tok(d),
        out_shape=jax.ShapeDtypeStruct((b, s, d), F32),
        scratch_shapes=[sq(F32), sq(BF16), sq(F32), sq(BF16), sq(BF16), sq(BF16),
                        pltpu.VMEM((GDN_HEADS, c, GDN_DV), F32), pltpu.VMEM((GDN_HEADS, c, GDN_DV), BF16),
                        pltpu.VMEM((c, GDN_VW), BF16), pltpu.VMEM((GDN_HEADS, GDN_DK, GDN_DV), F32)],
        compiler_params=_params(("parallel", "arbitrary")),
    )(q, k, v, gb, gz, h1, masks, o_norm, w_out, final_norm)


def _pad_cols(w, width):
    return jnp.pad(w, ((0, 0), (0, width - w.shape[1])))


def _rope_block(r, swapped):
    r1, r2 = r[:, :ROPE_HALF], r[:, ROPE_HALF:]
    z = jnp.zeros((r.shape[0], MLA_NOPE), r.dtype)
    body = jnp.concatenate([z, r2, r1] if swapped else [z, r1, r2], axis=1)
    return _pad_cols(body, LANES)


def _l0_weights(w_in, w_q_b, w_kv_b):
    o1 = MLA_Q_RANK
    o2 = o1 + MLA_KV_RANK
    o3 = o2 + MLA_ROPE
    kr = w_in[:, o2:o3]
    w_in_p = jnp.concatenate([w_in[:, :o2], _rope_block(kr, False), _rope_block(kr, True), w_in[:, o3:]], axis=1)
    qh = w_q_b.reshape(MLA_Q_RANK, MLA_HEADS, MLA_NOPE + MLA_ROPE)
    plain, swapped = [], []
    for h in range(MLA_HEADS):
        nope, rope = qh[:, h, :MLA_NOPE], qh[:, h, MLA_NOPE:]
        plain.append(_rope_block(rope, False).at[:, :MLA_NOPE].set(nope))
        swapped.append(_rope_block(rope, True))
    w_q_p = jnp.concatenate(plain + swapped, axis=1)
    kvh = w_kv_b.reshape(MLA_KV_RANK, MLA_HEADS, MLA_NOPE + MLA_V)
    k_cols = [_pad_cols(kvh[:, h, :MLA_NOPE], LANES) for h in range(MLA_HEADS)]
    v_cols = [_pad_cols(kvh[:, h, MLA_NOPE:], LANES) for h in range(MLA_HEADS)]
    w_kv_p = jnp.concatenate(k_cols + v_cols, axis=1)
    return w_in_p.astype(BF16), w_q_p.astype(BF16), w_kv_p.astype(BF16)


def kernel(x, positions, norm_ab, w_in_ab, q_a_norm, w_q_b, kv_a_norm, w_kv_b, pool_w, pool_scale, w_out_ab,
           norm_c, w_in_c, conv_w, a_log, dt_bias, o_norm, w_out_c, final_norm):
    assert norm_ab.shape[0] == 1 and norm_c.shape[0] == 1, "one even and one odd layer"
    b, s, d = x.shape

    cos, sin = _rope_tables(positions)
    ones = jnp.ones((b, s, MLA_NOPE), F32)
    tail = jnp.zeros((b, s, LANES - MLA_NOPE - MLA_ROPE), F32)
    cpat = jnp.concatenate([ones, cos, cos, tail], axis=-1)
    spat = jnp.concatenate([jnp.zeros_like(ones), sin, sin, tail], axis=-1)

    w_in_p, w_q_p, w_kv_p = _l0_weights(w_in_ab[0], w_q_b[0], w_kv_b[0])
    q, k, v, gz, yb = _l0_in(
        x, cpat, spat, norm_ab, w_in_p, q_a_norm, w_q_p, kv_a_norm, w_kv_p, pool_w[0].astype(BF16), pool_scale,
        scale=(MLA_NOPE + MLA_ROPE) ** -0.5 * math.log2(math.e))
    ya = _attention(q, k, v)

    h1, gq, gk, gv, gzc, gb = _mid(
        x, ya, gz, yb, w_out_ab[0].astype(BF16), norm_c, _pad_cols(w_in_c[0], GDN_CONV_CH + GDN_VW + LANES).astype(BF16),
        conv_w[0], _pad_cols(a_log, LANES), _pad_cols(dt_bias, LANES))
    return _gdn(gq, gk, gv, gb, gzc, h1, o_norm, w_out_c[0].astype(BF16), final_norm.reshape(1, -1))
```

```python
import functools
import math

import jax
import jax.numpy as jnp
from jax import lax
from jax.experimental import pallas as pl
from jax.experimental.pallas import tpu as pltpu

F32 = jnp.float32
BF16 = jnp.bfloat16

LANES = 128
RMS_EPS = 1e-6

MLA_HEADS = 8
MLA_Q_RANK = 256
MLA_KV_RANK = 128
MLA_NOPE = 64
MLA_ROPE = 32
MLA_V = 64
ROPE_THETA = 10000.0
ROPE_HALF = MLA_ROPE // 2

POOL_WINDOWS = (2, 4, 8, 16)
POOL_GROUP = 128
POOL_WIDTH = POOL_GROUP * len(POOL_WINDOWS)
POOL_HALO = 16

GDN_HEADS = 8
GDN_DK = 128
GDN_DV = 128
CONV_WIDTH = 4
CONV_HALO = 8
ATTN_GROUP = 4
ATTN_AHEAD = 3
MID_BLOCK = 512
GDN_CHUNK = 256
INV_BASE = 8
SOLVE_BLOCK = 64
GDN_QK = GDN_HEADS * GDN_DK
GDN_VW = GDN_HEADS * GDN_DV
GDN_CONV_CH = 2 * GDN_QK + GDN_VW

VMEM_LIMIT = 48 * 1024 * 1024
NEG_BIG = -1e30


def _tile(n, pref):
    t = min(n, pref)
    assert n % t == 0, (n, t)
    return t


def _params(sem):
    return pltpu.CompilerParams(dimension_semantics=sem, vmem_limit_bytes=VMEM_LIMIT)


def _silu(z):
    h = 0.5 * z
    return h + h * jnp.tanh(h)


def _rms(x, gain):
    ms = jnp.mean(x * x, axis=-1, keepdims=True)
    return x * lax.rsqrt(ms + RMS_EPS) * gain


def _dot(a, b):
    return jnp.dot(a, b, preferred_element_type=F32)


def _dot_nt(a, b):
    return lax.dot_general(a, b, (((1,), (1,)), ((), ())), preferred_element_type=F32)


def _dot_tn(a, b):
    return lax.dot_general(a, b, (((0,), (0,)), ((), ())), preferred_element_type=F32)


def _rope_table_kernel(pos_ref, freq_ref, cos_ref, sin_ref):
    ang = pos_ref[...].astype(F32) * freq_ref[...]
    cos_ref[...] = jnp.cos(ang)
    sin_ref[...] = jnp.sin(ang)


def _rope_tables(positions):
    b, s = positions.shape
    n = b * s * ROPE_HALF
    rows = n // LANES
    pos_rep = jnp.repeat(positions.reshape(-1), ROPE_HALF).reshape(rows, LANES)
    inv_freq = 1.0 / (ROPE_THETA ** (jnp.arange(ROPE_HALF, dtype=F32) / ROPE_HALF))
    freq_row = jnp.tile(inv_freq, LANES // ROPE_HALF).reshape(1, LANES)
    tr = _tile(rows, 512)
    cos, sin = pl.pallas_call(
        _rope_table_kernel,
        grid=(rows // tr,),
        in_specs=[pl.BlockSpec((tr, LANES), lambda i: (i, 0)),
                  pl.BlockSpec((1, LANES), lambda i: (0, 0))],
        out_specs=[pl.BlockSpec((tr, LANES), lambda i: (i, 0))] * 2,
        out_shape=[jax.ShapeDtypeStruct((rows, LANES), F32)] * 2,
        compiler_params=_params(("parallel",)),
    )(pos_rep, freq_row)
    return cos.reshape(b, s, ROPE_HALF), sin.reshape(b, s, ROPE_HALF)


def _l0_in_kernel(x_ref, cp_ref, sp_ref, nrm_ref, win_ref, qn_ref, wq_ref, kvn_ref, wkv_ref, pw_ref, ps_ref,
                  q_out, k_out, v_out, gz_out, yb_out, ext_ref, hn_ref, *, tm, scale):
    t = pl.program_id(1)

    @pl.when(t == 0)
    def _():
        ext_ref[0:POOL_HALO, :] = jnp.zeros((POOL_HALO, POOL_WIDTH), F32)

    hn_ref[...] = _rms(x_ref[0], nrm_ref[...]).astype(BF16)
    o1 = MLA_Q_RANK
    o2 = o1 + MLA_KV_RANK
    o3 = o2 + LANES
    o4 = o3 + LANES
    o5 = o4 + POOL_WIDTH
    o6 = o5 + MLA_HEADS * MLA_V
    hw = MLA_HEADS * LANES
    proj = lambda lo, hi: _dot(hn_ref[...], win_ref[:, lo:hi])

    lat = proj(0, o4)
    xp = proj(o4, o5)
    zb = proj(o6, o6 + POOL_WIDTH)
    qq = _dot(_rms(lat[:, :o1], qn_ref[...]).astype(BF16), wq_ref[...])

    ext_ref[POOL_HALO:POOL_HALO + tm, :] = xp
    row = lax.broadcasted_iota(jnp.int32, (tm, 1), 0) + t * tm
    for g, w in enumerate(POOL_WINDOWS):
        lo = g * POOL_GROUP
        acc = ext_ref[:, lo:lo + POOL_GROUP]
        span = 1
        while span < w:
            acc = acc + pltpu.roll(acc, span, axis=0)
            span *= 2
        cnt = jnp.minimum(row + 1, w).astype(F32)
        d = acc[POOL_HALO:, :] / cnt - xp[:, lo:lo + POOL_GROUP]
        y = _dot(d.astype(BF16), pw_ref[g]) * ps_ref[:, lo:lo + POOL_GROUP]
        yb_out[0, :, lo:lo + POOL_GROUP] = (y * _silu(zb[:, lo:lo + POOL_GROUP])).astype(BF16)
    ext_ref[0:POOL_HALO, :] = ext_ref[tm:tm + POOL_HALO, :]

    za = proj(o5, o6)

    lane = lax.broadcasted_iota(jnp.int32, (tm, LANES), 1)
    cos, sin = cp_ref[0], sp_ref[0]
    pad = jnp.zeros((tm, LANES - MLA_NOPE - MLA_ROPE), F32)
    cpat = jnp.concatenate([jnp.ones((tm, MLA_NOPE), F32), cos, cos, pad], axis=1)
    spat = jnp.concatenate([jnp.zeros((tm, MLA_NOPE), F32), -sin, sin, pad], axis=1)
    cq, sq = cpat * scale, spat * scale
    for h in range(MLA_HEADS):
        lo = h * LANES
        qh = qq[:, lo:lo + LANES] * cq + qq[:, hw + lo:hw + lo + LANES] * sq
        q_out[0, :, lo:lo + LANES] = qh.astype(BF16)

    kk = _dot(_rms(lat[:, o1:o2], kvn_ref[...]).astype(BF16), wkv_ref[...])
    gz_out[0] = _silu(za).astype(BF16)
    k_rope = lat[:, o2:o3] * cpat + lat[:, o3:o4] * spat
    for h in range(MLA_HEADS):
        lo = h * LANES
        k_out[0, :, lo:lo + LANES] = (kk[:, lo:lo + LANES] + k_rope).astype(BF16)
    ones_lane = jnp.where(lane == MLA_V, 1.0, 0.0)
    for h in range(MLA_HEADS):
        lo = h * LANES
        v_out[0, :, lo:lo + LANES] = (kk[:, hw + lo:hw + lo + LANES] + ones_lane).astype(BF16)


def _l0_in(x, cos, sin, norm, w_in, q_norm, w_q, kv_norm, w_kv, pool_w, pool_scale, *, scale):
    b, s, d = x.shape
    tm = _tile(s, 512)
    hw = MLA_HEADS * LANES
    vw = MLA_HEADS * MLA_V
    const = lambda shape: pl.BlockSpec(shape, lambda i, j: (0,) * len(shape))
    tok = lambda width: pl.BlockSpec((1, tm, width), lambda i, j: (i, j, 0))
    return pl.pallas_call(
        functools.partial(_l0_in_kernel, tm=tm, scale=scale),
        grid=(b, s // tm),
        in_specs=[tok(d), tok(ROPE_HALF), tok(ROPE_HALF), const(norm.shape), const(w_in.shape), const(q_norm.shape),
                  const(w_q.shape), const(kv_norm.shape), const(w_kv.shape), const(pool_w.shape),
                  const(pool_scale.shape)],
        out_specs=[tok(hw), tok(hw), tok(hw), tok(vw), tok(POOL_WIDTH)],
        out_shape=[jax.ShapeDtypeStruct((b, s, hw), BF16), jax.ShapeDtypeStruct((b, s, hw), BF16),
                   jax.ShapeDtypeStruct((b, s, hw), BF16), jax.ShapeDtypeStruct((b, s, vw), BF16),
                   jax.ShapeDtypeStruct((b, s, POOL_WIDTH), BF16)],
        scratch_shapes=[pltpu.VMEM((tm + POOL_HALO, POOL_WIDTH), F32), pltpu.VMEM((tm, d), BF16)],
        compiler_params=_params(("parallel", "arbitrary")),
    )(x, cos, sin, norm, w_in, q_norm, w_q, kv_norm, w_kv, pool_w, pool_scale)


def _attn_kernel(q_ref, k_ref, v_ref, o_ref, m_ref, acc_ref, *, tq):
    qi = pl.program_id(2)
    hcols = lambda hh: slice(hh * LANES, (hh + 1) * LANES)
    m_ref[...] = jnp.full(m_ref.shape, NEG_BIG, F32)
    acc_ref[...] = jnp.zeros(acc_ref.shape, F32)

    def blocks(items):
        rows = [pl.ds(pl.multiple_of(j * tq, tq), tq) for j, _ in items]
        work = [(bi, hh) for bi in range(len(items)) for hh in range(ATTN_GROUP)]
        score = lambda bi, hh: _dot_nt(q_ref[0, :, hcols(hh)], k_ref[0, rows[bi], hcols(hh)])
        s = [score(*w) for w in work[:ATTN_AHEAD]]
        for n, (bi, hh) in enumerate(work):
            if n + ATTN_AHEAD < len(work):
                s.append(score(*work[n + ATTN_AHEAD]))
            sh = s[n]
            if items[bi][1]:
                r = lax.broadcasted_iota(jnp.int32, (tq, tq), 0)
                c = lax.broadcasted_iota(jnp.int32, (tq, tq), 1)
                sh = jnp.where(c <= r, sh, NEG_BIG)
            m_old = m_ref[hh]
            m_new = jnp.maximum(m_old, jnp.max(sh, axis=-1, keepdims=True))
            p = jnp.concatenate([jnp.exp2(sh[:, c0:c0 + LANES] - m_new) for c0 in range(0, tq, LANES)], axis=1)
            acc_ref[hh] = (jnp.exp2(m_old - m_new) * acc_ref[hh]
                           + _dot(p.astype(BF16), v_ref[0, rows[bi], hcols(hh)]))
            m_ref[hh] = m_new

    def body(jj, _):
        blocks([(2 * jj, False), (2 * jj + 1, False)])
        return 0

    lax.fori_loop(0, qi // 2, body, 0)

    @pl.when(qi % 2 == 1)
    def _():
        blocks([(qi - 1, False), (qi, True)])

    @pl.when(qi % 2 == 0)
    def _():
        blocks([(qi, True)])

    lane = lax.broadcasted_iota(jnp.int32, (tq, LANES), 1)
    for hh in range(0, ATTN_GROUP, 2):
        lo, hi = [acc_ref[h] / acc_ref[h][:, MLA_V:MLA_V + 1] for h in (hh, hh + 1)]
        o_ref[0, :, hcols(hh // 2)] = jnp.where(lane < MLA_V, lo, pltpu.roll(hi, MLA_V, axis=1)).astype(BF16)


def _attention(q, k, v):
    b, s, hw = q.shape
    tq = _tile(s, 512)
    gw = ATTN_GROUP * LANES
    whole = pl.BlockSpec((1, s, gw), lambda i, p, j: (i, 0, p))
    return pl.pallas_call(
        functools.partial(_attn_kernel, tq=tq),
        grid=(b, MLA_HEADS // ATTN_GROUP, s // tq),
        in_specs=[pl.BlockSpec((1, tq, gw), lambda i, p, j: (i, j, p)), whole, whole],
        out_specs=pl.BlockSpec((1, tq, ATTN_GROUP * MLA_V), lambda i, p, j: (i, j, p)),
        out_shape=jax.ShapeDtypeStruct((b, s, MLA_HEADS * MLA_V), BF16),
        scratch_shapes=[pltpu.VMEM((ATTN_GROUP, tq, LANES), F32), pltpu.VMEM((ATTN_GROUP, tq, LANES), F32)],
        compiler_params=_params(("parallel", "parallel", "arbitrary")),
    )(q, k, v)


def _mid_kernel(x_ref, ya_ref, gz_ref, yb_ref, wo_ref, nrm_ref, win_ref, cw_ref, alog_ref, dtb_ref,
                h_out, q_out, k_out, v_out, gzc_out, gb_out, ext_ref, hn_ref, *, tm):
    t = pl.program_id(1)
    va = MLA_HEADS * MLA_V
    ya = (ya_ref[0].astype(F32) * gz_ref[0].astype(F32)).astype(BF16)
    h1 = x_ref[0] + _dot(ya, wo_ref[0:va, :]) + _dot(yb_ref[0], wo_ref[va:, :])
    h_out[0] = h1
    hn_ref[...] = _rms(h1, nrm_ref[...]).astype(BF16)

    @pl.when(t == 0)
    def _():
        ext_ref[0:CONV_HALO, :] = jnp.zeros((CONV_HALO, GDN_CONV_CH), F32)

    def l2(blk, scale=1.0):
        return blk * (lax.rsqrt(jnp.sum(blk * blk, axis=-1, keepdims=True) + RMS_EPS) * scale)

    def conv_block(lo, xc):
        cols = slice(lo, lo + MID_BLOCK)
        ext_ref[CONV_HALO:CONV_HALO + tm, cols] = xc
        xe = ext_ref[:, cols]
        conv = xc * cw_ref[CONV_WIDTH - 1:CONV_WIDTH, cols]
        for j in range(CONV_WIDTH - 1):
            back = CONV_WIDTH - 1 - j
            conv = conv + pltpu.roll(xe, back, axis=0)[CONV_HALO:, :] * cw_ref[j:j + 1, cols]
        ext_ref[0:CONV_HALO, cols] = ext_ref[tm:tm + CONV_HALO, cols]
        act = _silu(conv)
        for i in range(MID_BLOCK // LANES):
            c0 = lo + i * LANES
            blk = act[:, i * LANES:(i + 1) * LANES]
            if c0 < GDN_QK:
                q_out[0, :, c0:c0 + LANES] = l2(blk, GDN_DK ** -0.5).astype(BF16)
            elif c0 < 2 * GDN_QK:
                k_out[0, :, c0 - GDN_QK:c0 - GDN_QK + LANES] = l2(blk).astype(BF16)
            else:
                v_out[0, :, c0 - 2 * GDN_QK:c0 - 2 * GDN_QK + LANES] = blk.astype(BF16)

    def gate_block(lo, z):
        gzc_out[0, :, lo - GDN_CONV_CH:lo - GDN_CONV_CH + MID_BLOCK] = _silu(z).astype(BF16)

    def decay_block(lo, ab):
        sp_in = ab + dtb_ref[...]
        softplus = jnp.maximum(sp_in, 0.0) + jnp.log1p(jnp.exp(-jnp.abs(sp_in)))
        g = -jnp.exp(alog_ref[...]) * softplus
        beta = 1.0 / (1.0 + jnp.exp(-ab))
        lane = lax.broadcasted_iota(jnp.int32, (tm, LANES), 1)
        gb_out[0] = jnp.where(lane < GDN_HEADS, g, beta)

    o1 = GDN_CONV_CH + GDN_VW
    stages = [(lo, MID_BLOCK, conv_block) for lo in range(0, GDN_CONV_CH, MID_BLOCK)]
    stages += [(lo, MID_BLOCK, gate_block) for lo in range(GDN_CONV_CH, o1, MID_BLOCK)]
    stages += [(o1, LANES, decay_block)]
    pending = None
    for lo, width, consume in stages:
        proj = _dot(hn_ref[...], win_ref[:, lo:lo + width])
        if pending is not None:
            pending[0](pending[1], pending[2])
        pending = (consume, lo, proj)
    pending[0](pending[1], pending[2])


def _mid(x, ya, gz, yb, w_out, norm, w_in, conv_w, a_log, dt_bias):
    b, s, d = x.shape
    tm = _tile(s, 512)
    const = lambda shape: pl.BlockSpec(shape, lambda i, j: (0,) * len(shape), pipeline_mode=pl.Buffered(1))
    tok = lambda width: pl.BlockSpec((1, tm, width), lambda i, j: (i, j, 0))
    va = MLA_HEADS * MLA_V
    return pl.pallas_call(
        functools.partial(_mid_kernel, tm=tm),
        grid=(b, s // tm),
        in_specs=[tok(d), tok(va), tok(va), tok(POOL_WIDTH), const(w_out.shape), const(norm.shape),
                  const(w_in.shape), const(conv_w.shape), const(a_log.shape), const(dt_bias.shape)],
        out_specs=[tok(d), tok(GDN_QK), tok(GDN_QK), tok(GDN_VW), tok(GDN_VW), tok(LANES)],
        out_shape=[jax.ShapeDtypeStruct((b, s, d), F32), jax.ShapeDtypeStruct((b, s, GDN_QK), BF16),
                   jax.ShapeDtypeStruct((b, s, GDN_QK), BF16), jax.ShapeDtypeStruct((b, s, GDN_VW), BF16),
                   jax.ShapeDtypeStruct((b, s, GDN_VW), BF16), jax.ShapeDtypeStruct((b, s, LANES), F32)],
        scratch_shapes=[pltpu.VMEM((tm + CONV_HALO, GDN_CONV_CH), F32), pltpu.VMEM((tm, d), BF16)],
        compiler_params=_params(("parallel", "arbitrary")),
    )(x, ya, gz, yb, w_out, norm, w_in, conv_w, a_log, dt_bias)


def _chunk_cumsum(g):
    n = g.shape[0]
    row = lax.broadcasted_iota(jnp.int32, g.shape, 0)
    sh = 1
    while sh < n:
        g = g + jnp.where(row >= sh, pltpu.roll(g, sh, axis=0), 0.0)
        sh *= 2
    return g


def _inverse_masks(c):
    idx = jnp.arange(c)
    blk = lambda size: (idx[:, None] // size) == (idx[None, :] // size)
    masks = [blk(INV_BASE)]
    size = INV_BASE
    while size < SOLVE_BLOCK:
        masks.append(blk(2 * size) & ~blk(size) & (idx[:, None] > idx[None, :]))
        size *= 2
    return jnp.stack(masks).astype(F32)


def _gdn_kernel(q_ref, k_ref, v_ref, gb_ref, gz_ref, h_ref, msk_ref, on_ref, wo_ref, fn_ref, y_out,
                m_ref, m16_ref, t_ref, t16_ref, x_ref, at_ref, rhs_ref, vn_ref, og_ref, state_ref, *, c):
    heads = range(GDN_HEADS)
    hcols = lambda h: slice(h * LANES, (h + 1) * LANES)
    r = lax.broadcasted_iota(jnp.int32, (c, c), 0)
    col = lax.broadcasted_iota(jnp.int32, (c, c), 1)

    @pl.when(pl.program_id(1) == 0)
    def _():
        state_ref[...] = jnp.zeros(state_ref.shape, F32)

    gb = gb_ref[0]
    gc_all = _chunk_cumsum(gb)
    gc_t = gc_all.T
    gc_last = gc_all[c - 1:c, :]
    e_all = jnp.exp(gc_all)
    ek_all = jnp.exp(gc_last - gc_all)
    g_last = jnp.exp(gc_last)
    beta = lambda h: gb[:, GDN_HEADS + h:GDN_HEADS + h + 1]

    for h in heads:
        gamma = jnp.exp(jnp.where(r >= col, gc_all[:, h:h + 1] - gc_t[h:h + 1, :], -jnp.inf))
        k16 = k_ref[0, :, hcols(h)]
        kb16 = (k16.astype(F32) * beta(h)).astype(BF16)
        m = jnp.where(r > col, _dot_nt(kb16, k16) * gamma, 0.0)
        m_ref[h] = m
        m16_ref[h] = m.astype(BF16)
        at_ref[h] = (_dot_nt(q_ref[0, :, hcols(h)], k16) * gamma).astype(BF16)

    for h in heads:
        p = m_ref[h] * msk_ref[0]
        t_ref[h] = jnp.where(r == col, 1.0, 0.0) - p
        x_ref[h] = p.astype(BF16)
    span = 2
    while span < INV_BASE:
        for h in heads:
            p16 = x_ref[h]
            p = _dot(p16, p16)
            t = t_ref[h]
            t_ref[h] = t + _dot(t.astype(BF16), p.astype(BF16))
            x_ref[h] = p.astype(BF16)
        span *= 2
    for h in heads:
        t16_ref[h] = t_ref[h].astype(BF16)

    level = 1
    size = INV_BASE
    while size < SOLVE_BLOCK:
        for h in heads:
            off = (m_ref[h] * msk_ref[level]).astype(BF16)
            x_ref[h] = _dot(off, t16_ref[h]).astype(BF16)
        for h in heads:
            t = t_ref[h] - _dot(t16_ref[h], x_ref[h])
            t_ref[h] = t
            t16_ref[h] = t.astype(BF16)
        size *= 2
        level += 1

    st16 = [state_ref[h].astype(BF16) for h in heads]
    qs = []
    for h in heads:
        kf = k_ref[0, :, hcols(h)].astype(F32)
        kbe16 = (kf * (beta(h) * e_all[:, h:h + 1])).astype(BF16)
        rhs = v_ref[0, :, hcols(h)].astype(F32) * beta(h) - _dot(kbe16, st16[h])
        rhs_ref[h] = rhs
        x_ref[h, :, 0:GDN_DV] = rhs.astype(BF16)
        vn_ref[h] = jnp.zeros((c, GDN_DV), BF16)
        qd16 = (q_ref[0, :, hcols(h)].astype(F32) * e_all[:, h:h + 1]).astype(BF16)
        qs.append(_dot(qd16, st16[h]))

    for i in range(c // SOLVE_BLOCK):
        rows = slice(i * SOLVE_BLOCK, (i + 1) * SOLVE_BLOCK)
        if i > 0:
            for h in heads:
                rhs = rhs_ref[h, rows, :] - _dot(m16_ref[h, rows, :], vn_ref[h])
                x_ref[h, rows, 0:GDN_DV] = rhs.astype(BF16)
        for h in heads:
            vn_ref[h, rows, :] = _dot(t16_ref[h, rows, :], x_ref[h, :, 0:GDN_DV]).astype(BF16)

    for h in heads:
        o = qs[h] + _dot(at_ref[h], vn_ref[h])
        kd16 = (k_ref[0, :, hcols(h)].astype(F32) * ek_all[:, h:h + 1]).astype(BF16)
        state_ref[h] = state_ref[h] * g_last[:, h:h + 1] + _dot_tn(kd16, vn_ref[h])
        og_ref[:, hcols(h)] = (_rms(o, on_ref[...]) * gz_ref[0, :, hcols(h)].astype(F32)).astype(BF16)

    y_out[0] = _rms(h_ref[0] + _dot(og_ref[...], wo_ref[...]), fn_ref[...])


def _gdn(q, k, v, gb, gz, h1, o_norm, w_out, final_norm):
    b, s, d = h1.shape
    c = _tile(s, GDN_CHUNK)
    masks = _inverse_masks(c)
    tok = lambda width: pl.BlockSpec((1, c, width), lambda i, j: (i, j, 0))
    const = lambda shape: pl.BlockSpec(shape, lambda i, j: (0,) * len(shape), pipeline_mode=pl.Buffered(1))
    sq = lambda dt: pltpu.VMEM((GDN_HEADS, c, c), dt)
    return pl.pallas_call(
        functools.partial(_gdn_kernel, c=c),
        grid=(b, s // c),
        in_specs=[tok(GDN_QK), tok(GDN_QK), tok(GDN_VW), tok(LANES), tok(GDN_VW), tok(d),
                  const(masks.shape), const(o_norm.shape), const(w_out.shape), const(final_norm.shape)],
        out_specs=tok(d),
        out_shape=jax.ShapeDtypeStruct((b, s, d), F32),
        scratch_shapes=[sq(F32), sq(BF16), sq(F32), sq(BF16), sq(BF16), sq(BF16),
                        pltpu.VMEM((GDN_HEADS, c, GDN_DV), F32), pltpu.VMEM((GDN_HEADS, c, GDN_DV), BF16),
                        pltpu.VMEM((c, GDN_VW), BF16), pltpu.VMEM((GDN_HEADS, GDN_DK, GDN_DV), F32)],
        compiler_params=_params(("parallel", "arbitrary")),
    )(q, k, v, gb, gz, h1, masks, o_norm, w_out, final_norm)


def _pad_cols(w, width):
    return jnp.pad(w, ((0, 0), (0, width - w.shape[1])))


def _rope_block(r, swapped):
    r1, r2 = r[:, :ROPE_HALF], r[:, ROPE_HALF:]
    z = jnp.zeros((r.shape[0], MLA_NOPE), r.dtype)
    body = jnp.concatenate([z, r2, r1] if swapped else [z, r1, r2], axis=1)
    return _pad_cols(body, LANES)


def _l0_weights(w_in, w_q_b, w_kv_b):
    o1 = MLA_Q_RANK
    o2 = o1 + MLA_KV_RANK
    o3 = o2 + MLA_ROPE
    kr = w_in[:, o2:o3]
    w_in_p = jnp.concatenate([w_in[:, :o2], _rope_block(kr, False), _rope_block(kr, True), w_in[:, o3:]], axis=1)
    qh = w_q_b.reshape(MLA_Q_RANK, MLA_HEADS, MLA_NOPE + MLA_ROPE)
    plain, swapped = [], []
    for h in range(MLA_HEADS):
        nope, rope = qh[:, h, :MLA_NOPE], qh[:, h, MLA_NOPE:]
        plain.append(_rope_block(rope, False).at[:, :MLA_NOPE].set(nope))
        swapped.append(_rope_block(rope, True))
    w_q_p = jnp.concatenate(plain + swapped, axis=1)
    kvh = w_kv_b.reshape(MLA_KV_RANK, MLA_HEADS, MLA_NOPE + MLA_V)
    k_cols = [_pad_cols(kvh[:, h, :MLA_NOPE], LANES) for h in range(MLA_HEADS)]
    v_cols = [_pad_cols(kvh[:, h, MLA_NOPE:], LANES) for h in range(MLA_HEADS)]
    w_kv_p = jnp.concatenate(k_cols + v_cols, axis=1)
    return w_in_p.astype(BF16), w_q_p.astype(BF16), w_kv_p.astype(BF16)


def kernel(x, positions, norm_ab, w_in_ab, q_a_norm, w_q_b, kv_a_norm, w_kv_b, pool_w, pool_scale, w_out_ab,
           norm_c, w_in_c, conv_w, a_log, dt_bias, o_norm, w_out_c, final_norm):
    assert norm_ab.shape[0] == 1 and norm_c.shape[0] == 1, "one even and one odd layer"
    b, s, d = x.shape

    cos, sin = _rope_tables(positions)
    w_in_p, w_q_p, w_kv_p = _l0_weights(w_in_ab[0], w_q_b[0], w_kv_b[0])
    q, k, v, gz, yb = _l0_in(
        x, cos, sin, norm_ab, w_in_p, q_a_norm, w_q_p, kv_a_norm, w_kv_p, pool_w[0].astype(BF16), pool_scale,
        scale=(MLA_NOPE + MLA_ROPE) ** -0.5 * math.log2(math.e))
    ya = _attention(q, k, v)

    h1, gq, gk, gv, gzc, gb = _mid(
        x, ya, gz, yb, w_out_ab[0].astype(BF16), norm_c, _pad_cols(w_in_c[0], GDN_CONV_CH + GDN_VW + LANES).astype(BF16),
        conv_w[0], _pad_cols(a_log, LANES), _pad_cols(dt_bias, LANES))
    return _gdn(gq, gk, gv, gb, gzc, h1, o_norm, w_out_c[0].astype(BF16), final_norm.reshape(1, -1))
```
